```python
import math
import jax, jax.numpy as jnp
from jax import lax
import numpy as np

D_MODEL = 1024
BATCH = 16
SEQ = 4096
DEPTH = 1
DEC_BATCH = 8
DEC_SEQ = 32
PAST_LEN = 4096

CHUNK = 64
Q_BLOCK = 128
HEAD_DIM = 64
FOX_HEADS = 8
DSA_HEADS = 8
DSA_KV_HEADS = 2
IDX_HEADS = 4
IDX_DIM = 64
IDX_TOPK_MAX = 256
ROPE_THETA = 500000.0
ROT_DIM = HEAD_DIM // 4
FOX_WIDTH = FOX_HEADS * HEAD_DIM
DSA_WIDTH = DSA_HEADS * HEAD_DIM
DSA_KV_WIDTH = DSA_KV_HEADS * HEAD_DIM
MIX_WIDTH = FOX_WIDTH + DSA_WIDTH
D_FF = -(-8 * D_MODEL // (3 * 256)) * 256
ALPHA = (2 * DEPTH) ** 0.25
BETA = (8 * DEPTH) ** -0.25
LN_EPS = 1e-5
FORGET_BIAS_INIT = 3.0
IN_SIZES = (FOX_WIDTH, FOX_WIDTH, FOX_WIDTH, FOX_HEADS,
            DSA_WIDTH, DSA_KV_WIDTH, DSA_KV_WIDTH, IDX_HEADS * IDX_DIM, IDX_DIM, IDX_HEADS)
IN_WIDTH = sum(IN_SIZES)
IN_VALUE_PARTS = (2, 6)

kernel_name = "fox_dsa_hybrid_stream_step"


def layer_norm(x, g, b):
    xf = x.astype(jnp.float32)
    mu = jnp.mean(xf, -1, keepdims=True)
    var = jnp.mean(jnp.square(xf - mu), -1, keepdims=True)
    return ((xf - mu) * lax.rsqrt(var + LN_EPS) * g + b).astype(x.dtype)


def rope_tables(positions):
    half = ROT_DIM // 2
    inv_freq = ROPE_THETA ** (-jnp.arange(half, dtype=jnp.float32) * 2.0 / ROT_DIM)
    ang = positions.astype(jnp.float32)[:, None] * inv_freq[None, :]
    return jnp.cos(ang), jnp.sin(ang)


def partial_rope(x, cos, sin):
    half = ROT_DIM // 2
    x1, x2 = x[..., :half], x[..., half:ROT_DIM]
    cos = cos.astype(x.dtype)
    sin = sin.astype(x.dtype)
    return jnp.concatenate([x1 * cos - x2 * sin, x2 * cos + x1 * sin, x[..., ROT_DIM:]], axis=-1)


def project_inputs(x, positions, w_in, b_forget):
    B, S, _ = x.shape
    points, acc = [], 0
    for size in IN_SIZES[:-1]:
        acc += size
        points.append(acc)
    fq, fk, fv, ff, dq, dk, dv, iq, ik, iw = jnp.split(x @ w_in, points, axis=-1)
    cos, sin = rope_tables(positions)
    c4, s4 = cos[None, :, None, :], sin[None, :, None, :]
    fox_q = fq.reshape(B, S, FOX_HEADS, HEAD_DIM)
    fox_k = fk.reshape(B, S, FOX_HEADS, HEAD_DIM)
    fox_v = fv.reshape(B, S, FOX_HEADS, HEAD_DIM)
    fox_logf = jax.nn.log_sigmoid((ff + b_forget).astype(jnp.float32))
    dsa_q = partial_rope(dq.reshape(B, S, DSA_HEADS, HEAD_DIM), c4, s4)
    dsa_k = partial_rope(dk.reshape(B, S, DSA_KV_HEADS, HEAD_DIM), c4, s4)
    dsa_v = dv.reshape(B, S, DSA_KV_HEADS, HEAD_DIM)
    idx_q = partial_rope(iq.reshape(B, S, IDX_HEADS, IDX_DIM), c4, s4)
    idx_k = partial_rope(ik, cos[None], sin[None])
    return fox_q, fox_k, fox_v, fox_logf, dsa_q, dsa_k, dsa_v, idx_q, idx_k, iw


def fox_prompt(q, k, v, logf):
    B, S, H, Dh = q.shape
    ct = jnp.cumsum(logf, axis=1).transpose(0, 2, 1)
    kpos = jnp.arange(S)
    scale = Dh ** -0.5

    def block(i):
        start = i * Q_BLOCK
        qb = lax.dynamic_slice_in_dim(q, start, Q_BLOCK, axis=1)
        cq = lax.dynamic_slice_in_dim(ct, start, Q_BLOCK, axis=2)
        s = jnp.einsum('bqhd,bshd->bhqs', qb, k).astype(jnp.float32) * scale
        s = s + cq[..., None] - ct[:, :, None, :]
        qpos = start + jnp.arange(Q_BLOCK)
        s = jnp.where(kpos[None, :] <= qpos[:, None], s, -jnp.inf)
        p = jax.nn.softmax(s, axis=-1).astype(v.dtype)
        return jnp.einsum('bhqs,bshd->bqhd', p, v)

    out = lax.map(block, jnp.arange(S // Q_BLOCK))
    return jnp.moveaxis(out, 0, 1).reshape(B, S, H * Dh)


def fox_sample(q, k_new, v_new, lf_new, k_past, v_past, lf_past):
    B, T, H, Dh = q.shape
    P = k_past.shape[1]
    k = jnp.concatenate([k_past, k_new], axis=1)
    v = jnp.concatenate([v_past, v_new], axis=1)
    lp = lf_past.astype(jnp.float32)
    d_past = lax.cumsum(lp, axis=1, reverse=True) - lp
    c_new = jnp.cumsum(lf_new.astype(jnp.float32), axis=1)
    key_bias = jnp.concatenate([d_past, -c_new], axis=1)
    s = jnp.einsum('bqhd,bshd->bhqs', q, k).astype(jnp.float32) * Dh ** -0.5
    s = s + c_new.transpose(0, 2, 1)[..., None] + key_bias.transpose(0, 2, 1)[:, :, None, :]
    kpos = jnp.arange(P + T)
    qpos = P + jnp.arange(T)
    s = jnp.where(kpos[None, :] <= qpos[:, None], s, -jnp.inf)
    p = jax.nn.softmax(s, axis=-1).astype(v.dtype)
    return jnp.einsum('bhqs,bshd->bqhd', p, v).reshape(B, T, H * Dh)


def gathered_attention(q, k, v, idx, valid):
    B, Q, H, Dh = q.shape
    KVH = k.shape[2]
    gather = jax.vmap(lambda t, i: t[i])
    ks = gather(k, idx)
    vs = gather(v, idx)
    qg = q.reshape(B, Q, KVH, H // KVH, Dh)
    s = jnp.einsum('bqhgd,bqnhd->bqhgn', qg, ks).astype(jnp.float32) * Dh ** -0.5
    s = jnp.where(valid[:, :, None, None, :], s, -jnp.inf)
    p = jax.nn.softmax(s, axis=-1).astype(v.dtype)
    return jnp.einsum('bqhgn,bqnhd->bqhgd', p, vs).reshape(B, Q, H * Dh)


def dsa_select_attend(q, iq, iw, q_pos, k, v, ik, k_pos, top_k):
    logits = jnp.einsum('bqhd,bsd->bqhs', iq, ik)
    score = jnp.einsum('bqh,bqhs->bqs', iw, jax.nn.relu(logits)).astype(jnp.float32)
    q_chunk = q_pos // CHUNK
    adm = (k_pos[None, :] // CHUNK) <= q_chunk[:, None]
    score = jnp.where(adm[None], score, -jnp.inf)
    _, idx = lax.top_k(score, top_k)
    valid = (k_pos[idx] // CHUNK) <= q_chunk[None, :, None]
    return gathered_attention(q, k, v, idx, valid)


def dsa_prompt(q, iq, iw, k, v, ik, top_k):
    B, S = q.shape[:2]
    kpos = jnp.arange(S)

    def block(i):
        start = i * Q_BLOCK
        sl = lambda t: lax.dynamic_slice_in_dim(t, start, Q_BLOCK, axis=1)
        return dsa_select_attend(sl(q), sl(iq), sl(iw), start + jnp.arange(Q_BLOCK), k, v, ik, kpos, top_k)

    out = lax.map(block, jnp.arange(S // Q_BLOCK))
    return jnp.moveaxis(out, 0, 1).reshape(B, S, -1)


def post_sublayers(x, mix, ln1_g, ln1_b, w_gate, w_up, w_down, ln2_g, ln2_b):
    h = layer_norm(ALPHA * x + mix, ln1_g, ln1_b)
    f = (jax.nn.silu(h @ w_gate) * (h @ w_up)) @ w_down
    return layer_norm(ALPHA * h + f, ln2_g, ln2_b)


def setup_inputs(seed: int = 0) -> dict:
    key = jax.random.key(seed)
    ks = jax.random.split(key, 20)
    f32 = jnp.float32
    n = lambda k, s: jax.random.normal(k, s, f32)
    x_prompt = n(ks[0], (BATCH, SEQ, D_MODEL))
    x_sample = n(ks[1], (DEC_BATCH, DEC_SEQ, D_MODEL))
    cache_fox_k = n(ks[2], (DEPTH, DEC_BATCH, PAST_LEN, FOX_HEADS, HEAD_DIM))
    cache_fox_v = n(ks[3], (DEPTH, DEC_BATCH, PAST_LEN, FOX_HEADS, HEAD_DIM)) * BETA
    cache_fox_logf = jax.nn.log_sigmoid(FORGET_BIAS_INIT + n(ks[4], (DEPTH, DEC_BATCH, PAST_LEN, FOX_HEADS)))
    cache_dsa_k = n(ks[5], (DEPTH, DEC_BATCH, PAST_LEN, DSA_KV_HEADS, HEAD_DIM))
    cache_dsa_v = n(ks[6], (DEPTH, DEC_BATCH, PAST_LEN, DSA_KV_HEADS, HEAD_DIM)) * BETA
    cache_idx_k = n(ks[7], (DEPTH, DEC_BATCH, PAST_LEN, IDX_DIM))
    col_scale = jnp.concatenate([jnp.full((sz,), BETA if i in IN_VALUE_PARTS else 1.0, f32)
                                 for i, sz in enumerate(IN_SIZES)])
    w_in = n(ks[8], (DEPTH, D_MODEL, IN_WIDTH)) * D_MODEL ** -0.5 * col_scale
    b_forget = FORGET_BIAS_INIT + 0.5 * n(ks[9], (DEPTH, FOX_HEADS))
    w_out = n(ks[10], (DEPTH, MIX_WIDTH, D_MODEL)) * MIX_WIDTH ** -0.5 * BETA
    ln1_g = 1.0 + 0.05 * n(ks[11], (DEPTH, D_MODEL))
    ln1_b = 0.02 * n(ks[12], (DEPTH, D_MODEL))
    w_gate = n(ks[13], (DEPTH, D_MODEL, D_FF)) * D_MODEL ** -0.5
    w_up = n(ks[14], (DEPTH, D_MODEL, D_FF)) * D_MODEL ** -0.5 * BETA
    w_down = n(ks[15], (DEPTH, D_FF, D_MODEL)) * D_FF ** -0.5 * BETA
    ln2_g = 1.0 + 0.05 * n(ks[16], (DEPTH, D_MODEL))
    ln2_b = 0.02 * n(ks[17], (DEPTH, D_MODEL))
    return {"x_prompt": x_prompt, "x_sample": x_sample,
            "cache_fox_k": cache_fox_k, "cache_fox_v": cache_fox_v, "cache_fox_logf": cache_fox_logf,
            "cache_dsa_k": cache_dsa_k, "cache_dsa_v": cache_dsa_v, "cache_idx_k": cache_idx_k,
            "w_in": w_in, "b_forget": b_forget, "w_out": w_out, "ln1_g": ln1_g, "ln1_b": ln1_b,
            "w_gate": w_gate, "w_up": w_up, "w_down": w_down, "ln2_g": ln2_g, "ln2_b": ln2_b}


def reference(x_prompt, x_sample, cache_fox_k, cache_fox_v, cache_fox_logf, cache_dsa_k, cache_dsa_v,
              cache_idx_k, w_in, b_forget, w_out, ln1_g, ln1_b, w_gate, w_up, w_down, ln2_g, ln2_b):
    S = x_prompt.shape[1]
    T = x_sample.shape[1]
    P = cache_fox_k.shape[2]
    topk_prompt = min(IDX_TOPK_MAX, S // 4)
    topk_sample = min(IDX_TOPK_MAX, (P + T) // 4)
    pos_p = jnp.arange(S)
    pos_s = P + jnp.arange(T)
    kpos_s = jnp.arange(P + T)
    yp, ys = x_prompt, x_sample
    p_st = [[] for _ in range(6)]
    s_st = [[] for _ in range(6)]
    for l in range(DEPTH):
        fq, fk, fv, flf, dq, dk, dv, iq, ik, iw = project_inputs(yp, pos_p, w_in[l], b_forget[l])
        fox_o = fox_prompt(fq, fk, fv, flf)
        dsa_o = dsa_prompt(dq, iq, iw, dk, dv, ik, topk_prompt)
        mix = jnp.concatenate([fox_o, dsa_o], axis=-1) @ w_out[l]
        yp = post_sublayers(yp, mix, ln1_g[l], ln1_b[l], w_gate[l], w_up[l], w_down[l], ln2_g[l], ln2_b[l])
        for lst, arr in zip(p_st, (fk, fv, flf, dk, dv, ik)):
            lst.append(arr)
        gq, gk, gv, glf, eq, ek, ev, jq, jk, jw = project_inputs(ys, pos_s, w_in[l], b_forget[l])
        fox_s = fox_sample(gq, gk, gv, glf, cache_fox_k[l], cache_fox_v[l], cache_fox_logf[l])
        k_all = jnp.concatenate([cache_dsa_k[l], ek], axis=1)
        v_all = jnp.concatenate([cache_dsa_v[l], ev], axis=1)
        ik_all = jnp.concatenate([cache_idx_k[l], jk], axis=1)
        dsa_s = dsa_select_attend(eq, jq, jw, pos_s, k_all, v_all, ik_all, kpos_s, topk_sample)
        mix_s = jnp.concatenate([fox_s, dsa_s], axis=-1) @ w_out[l]
        ys = post_sublayers(ys, mix_s, ln1_g[l], ln1_b[l], w_gate[l], w_up[l], w_down[l], ln2_g[l], ln2_b[l])
        for lst, arr in zip(s_st, (gk, gv, glf, ek, ev, jk)):
            lst.append(arr)
    p_fox_k, p_fox_v, p_fox_logf, p_dsa_k, p_dsa_v, p_idx_k = [jnp.stack(a, axis=0) for a in p_st]
    s_fox_k, s_fox_v, s_fox_logf, s_dsa_k, s_dsa_v, s_idx_k = [jnp.stack(a, axis=0) for a in s_st]
    return (yp, ys, p_fox_k, p_fox_v, p_fox_logf, p_dsa_k, p_dsa_v, p_idx_k,
            s_fox_k, s_fox_v, s_fox_logf, s_dsa_k, s_dsa_v, s_idx_k)
```

```python
import functools

import numpy as np
import jax
import jax.numpy as jnp
from jax import lax
from jax.experimental import pallas as pl
from jax.experimental.pallas import tpu as pltpu

HEAD_DIM = 64
FOX_HEADS = 8
DSA_HEADS = 8
DSA_KV_HEADS = 2
IDX_HEADS = 4
CHUNK = 64
CHUNK_SHIFT = 6
IDX_TOPK_MAX = 256
ROPE_THETA = 500000.0
ROT_DIM = HEAD_DIM // 4
LN_EPS = 1e-5

LANES = 128
FOX_WIDTH = FOX_HEADS * HEAD_DIM
DSA_WIDTH = DSA_HEADS * HEAD_DIM
INT_MIN = -(2 ** 31)
NEG_INF = float("-inf")
F32 = jnp.float32
BF16 = jnp.bfloat16
VMEM_LIMIT = 56 * 1024 * 1024

_C_FQ, _C_FK, _C_FV, _C_DQ = 0, 512, 1024, 1536
_C_DK, _C_DV, _C_IQ, _C_IK, _C_MISC, _C_END = 2048, 2176, 2304, 2560, 2688, 2816
_MISC_IW = 8


def _nt_dot(a, b):
    return lax.dot_general(a, b, (((1,), (1,)), ((), ())), preferred_element_type=F32)


def _half_mask(shape, hh):
    lane = lax.broadcasted_iota(jnp.int32, shape, len(shape) - 1)
    return (lane < HEAD_DIM) if hh == 0 else (lane >= HEAD_DIM)


def _project_kernel(x_ref, w_ref, bias_ref, c_ref, sa_ref, sb_ref,
                    fq_ref, fk_ref, fkb_ref, fv_ref, fvb_ref, dq_ref, dk_ref, dv_ref,
                    dk2_ref, dv2_ref, iq_ref, ik_ref, ik2_ref, misc_ref):
    xb = x_ref[...].astype(BF16)
    cos, sa, sb = c_ref[...], sa_ref[...], sb_ref[...]

    def mm(lo, hi):
        return jnp.dot(xb, w_ref[:, lo:hi], preferred_element_type=F32)

    def rope(yb):
        return yb * cos + pltpu.roll(yb, LANES - ROT_DIM // 2, 1) * sa + pltpu.roll(yb, ROT_DIM // 2, 1) * sb

    def blocks(y):
        return [y[:, b * LANES:(b + 1) * LANES] for b in range(y.shape[1] // LANES)]

    fq_ref[...] = mm(_C_FQ, _C_FK).astype(BF16)
    y = mm(_C_FK, _C_FV)
    fk_ref[...] = y
    fkb_ref[...] = y.astype(BF16)
    y = mm(_C_FV, _C_DQ)
    fv_ref[...] = y
    fvb_ref[...] = y.astype(BF16)
    for b, yb in enumerate(blocks(mm(_C_DQ, _C_DK))):
        dq_ref[:, b * LANES:(b + 1) * LANES] = rope(yb).astype(BF16)

    half0 = _half_mask((x_ref.shape[0], LANES), 0)
    kv = mm(_C_DK, _C_IQ)
    k = rope(kv[:, :LANES])
    v = kv[:, LANES:]
    dk_ref[...] = k
    dv_ref[...] = v
    for src, dst in ((k, dk2_ref), (v, dv2_ref)):
        sw = pltpu.roll(src, HEAD_DIM, 1)
        dst[:, :LANES] = jnp.where(half0, src, sw).astype(BF16)
        dst[:, LANES:] = jnp.where(half0, sw, src).astype(BF16)

    for b, yb in enumerate(blocks(mm(_C_IQ, _C_IK))):
        iq_ref[:, b * LANES:(b + 1) * LANES] = rope(yb).astype(BF16)
    ik = rope(mm(_C_IK, _C_MISC))
    ik_ref[...] = ik[:, :HEAD_DIM]
    ik2_ref[...] = ik.astype(BF16)

    z = mm(_C_MISC, _C_END)
    zf = z + bias_ref[...]
    logf = jnp.minimum(zf, 0.0) - jnp.log1p(jnp.exp(-jnp.abs(zf)))
    lane = lax.broadcasted_iota(jnp.int32, z.shape, 1)
    misc_ref[...] = jnp.where(lane < _MISC_IW, logf, z)


def _pack_w_in(w, b_forget):
    d = w.shape[0]
    sizes = (FOX_WIDTH, FOX_WIDTH, FOX_WIDTH, FOX_HEADS, DSA_WIDTH, DSA_KV_HEADS * HEAD_DIM,
             DSA_KV_HEADS * HEAD_DIM, IDX_HEADS * HEAD_DIM, HEAD_DIM, IDX_HEADS)
    offs = np.concatenate([[0], np.cumsum(sizes)])
    fq, fk, fv, ff, dq, dk, dv, iq, ik, iw = [w[:, offs[i]:offs[i + 1]] for i in range(len(sizes))]
    scale = HEAD_DIM ** -0.5
    misc = jnp.concatenate([ff, iw, jnp.zeros((d, LANES - FOX_HEADS - IDX_HEADS), w.dtype)], axis=1)
    packed = jnp.concatenate([fq * scale, fk, fv, dq * scale, dk, dv, iq, ik, ik, misc], axis=1)
    bias = jnp.concatenate([b_forget.astype(F32), jnp.zeros((LANES - FOX_HEADS,), F32)])[None, :]
    return packed.astype(BF16), bias


def _rope_lane_tables(positions):
    half = ROT_DIM // 2
    inv_freq = ROPE_THETA ** (-jnp.arange(half, dtype=F32) * 2.0 / ROT_DIM)
    ang = positions.astype(F32)[:, None] * inv_freq[None, :]
    cos, sin = jnp.cos(ang), jnp.sin(ang)
    n = positions.shape[0]
    rest = HEAD_DIM - ROT_DIM
    c = jnp.concatenate([cos, cos, jnp.ones((n, rest), F32)], axis=1)
    sa = jnp.concatenate([-sin, jnp.zeros((n, half + rest), F32)], axis=1)
    sb = jnp.concatenate([jnp.zeros((n, half), F32), sin, jnp.zeros((n, rest), F32)], axis=1)
    rep = LANES // HEAD_DIM
    return tuple(jnp.tile(t, (1, rep)) for t in (c, sa, sb))


def _project(x2, w_packed, bias, tables, seq, tm):
    n, d = x2.shape
    assert seq % tm == 0 and n % seq == 0
    tiles_per_seq = seq // tm
    row = lambda t: (t, 0)
    pos = lambda t: (t % tiles_per_seq, 0)
    const = lambda t: (0, 0)

    def out(width, dtype):
        return jax.ShapeDtypeStruct((n, width), dtype), pl.BlockSpec((tm, width), row)

    outs = [out(FOX_WIDTH, BF16),
            out(FOX_WIDTH, F32), out(FOX_WIDTH, BF16),
            out(FOX_WIDTH, F32), out(FOX_WIDTH, BF16),
            out(DSA_WIDTH, BF16),
            out(LANES, F32), out(LANES, F32),
            out(2 * LANES, BF16), out(2 * LANES, BF16),
            out(IDX_HEADS * HEAD_DIM, BF16),
            out(HEAD_DIM, F32), out(LANES, BF16),
            out(LANES, F32)]
    return pl.pallas_call(
        _project_kernel,
        grid=(n // tm,),
        in_specs=[pl.BlockSpec((tm, d), row),
                  pl.BlockSpec((d, _C_END), const, pipeline_mode=pl.Buffered(1)),
                  pl.BlockSpec((1, LANES), const),
                  pl.BlockSpec((tm, LANES), pos), pl.BlockSpec((tm, LANES), pos), pl.BlockSpec((tm, LANES), pos)],
        out_specs=[o[1] for o in outs],
        out_shape=[o[0] for o in outs],
        compiler_params=pltpu.CompilerParams(dimension_semantics=("parallel",), vmem_limit_bytes=VMEM_LIMIT),
        name="project",
    )(x2, w_packed, bias, *tables)


def _cumsum_kernel(x_ref, o_ref, carry_ref, *, reverse, inclusive):
    @pl.when(pl.program_id(1) == 0)
    def _():
        carry_ref[...] = jnp.zeros_like(carry_ref)

    x = x_ref[0]
    tm = x.shape[0]
    hi = x.astype(BF16)
    r1 = x - hi.astype(F32)
    mid = r1.astype(BF16)
    lo = (r1 - mid.astype(F32)).astype(BF16)
    row = lax.broadcasted_iota(jnp.int32, (tm, tm), 0)
    col = lax.broadcasted_iota(jnp.int32, (tm, tm), 1)
    if reverse:
        keep = (col >= row) if inclusive else (col > row)
    else:
        keep = (col <= row) if inclusive else (col < row)
    tri = jnp.where(keep, 1.0, 0.0).astype(BF16)
    cs = (jnp.dot(tri, hi, preferred_element_type=F32) + jnp.dot(tri, mid, preferred_element_type=F32)
          + jnp.dot(tri, lo, preferred_element_type=F32))
    carry = carry_ref[0:1, :]
    o_ref[0] = cs + carry
    carry_ref[0:1, :] = carry + jnp.sum(x, axis=0, keepdims=True)


def _cumsum(x, *, reverse=False, inclusive=True, tm=512):
    b, s, w = x.shape
    tm = min(tm, s)
    assert s % tm == 0 and w == LANES
    nt = s // tm
    idx = (lambda i, t: (i, nt - 1 - t, 0)) if reverse else (lambda i, t: (i, t, 0))
    return pl.pallas_call(
        functools.partial(_cumsum_kernel, reverse=reverse, inclusive=inclusive),
        grid=(b, nt),
        in_specs=[pl.BlockSpec((1, tm, w), idx)],
        out_specs=pl.BlockSpec((1, tm, w), idx),
        out_shape=jax.ShapeDtypeStruct(x.shape, F32),
        scratch_shapes=[pltpu.VMEM((8, w), F32)],
        compiler_params=pltpu.CompilerParams(dimension_semantics=("parallel", "arbitrary")),
        name="cumsum",
    )(x)


def _fox_kernel(it_ref, jt_ref, q_ref, k_ref, v_ref, kb_ref, o_ref, m_ref, l_ref, acc_ref,
                *, tq, tk, q_pos0, nk):
    t = pl.program_id(2)
    i = it_ref[t]
    j = jt_ref[t]
    row0 = q_pos0 + i * tq
    j_last = jnp.minimum(nk - 1, (row0 + tq - 1) // tk)

    @pl.when(j == 0)
    def _():
        m_ref[...] = jnp.full_like(m_ref, NEG_INF)
        l_ref[...] = jnp.zeros_like(l_ref)
        acc_ref[...] = jnp.zeros_like(acc_ref)

    def step(masked):
        qp, kp, vp = q_ref[0], k_ref[0], v_ref[0]
        for hh in range(2):
            qh = jnp.where(_half_mask(qp.shape, hh), qp, jnp.zeros_like(qp))
            s = _nt_dot(qh, kp) + kb_ref[0, 0, hh:hh + 1, :]
            if masked:
                row = row0 + lax.broadcasted_iota(jnp.int32, (tq, tk), 0)
                col = j * tk + lax.broadcasted_iota(jnp.int32, (tq, tk), 1)
                s = jnp.where(col <= row, s, NEG_INF)
            m_prev = m_ref[hh]
            m_new = jnp.maximum(m_prev, jnp.max(s, axis=1, keepdims=True))
            alpha = jnp.exp(m_prev - m_new)
            p = jnp.exp(s - m_new)
            l_ref[hh] = alpha * l_ref[hh] + jnp.sum(p, axis=1, keepdims=True)
            acc_ref[hh] = alpha * acc_ref[hh] + jnp.dot(p.astype(BF16), vp, preferred_element_type=F32)
            m_ref[hh] = m_new

    crosses = j * tk + tk - 1 > row0

    @pl.when(crosses)
    def _():
        step(True)

    @pl.when(jnp.logical_not(crosses))
    def _():
        step(False)

    @pl.when(j == j_last)
    def _():
        o0 = acc_ref[0] / l_ref[0]
        o1 = acc_ref[1] / l_ref[1]
        o_ref[0] = jnp.where(_half_mask(o0.shape, 0), o0, o1).astype(o_ref.dtype)


def _fox_attn(q, k, v, kb, *, q_pos0, tq, tk):
    b, tq_all, _ = q.shape
    tk_all = k.shape[1]
    assert tq_all % tq == 0 and tk_all % tk == 0
    nq, nk = tq_all // tq, tk_all // tk
    it, jt = [], []
    for i in range(nq):
        for j in range(min(nk - 1, (q_pos0 + i * tq + tq - 1) // tk) + 1):
            it.append(i)
            jt.append(j)
    it = jnp.asarray(np.asarray(it, np.int32))
    jt = jnp.asarray(np.asarray(jt, np.int32))
    kb4 = kb.reshape(b, FOX_HEADS // 2, 2, tk_all)
    grid_spec = pltpu.PrefetchScalarGridSpec(
        num_scalar_prefetch=2,
        grid=(b, FOX_HEADS // 2, int(it.shape[0])),
        in_specs=[pl.BlockSpec((1, tq, LANES), lambda bb, p, t, it_, jt_: (bb, it_[t], p)),
                  pl.BlockSpec((1, tk, LANES), lambda bb, p, t, it_, jt_: (bb, jt_[t], p)),
                  pl.BlockSpec((1, tk, LANES), lambda bb, p, t, it_, jt_: (bb, jt_[t], p)),
                  pl.BlockSpec((1, 1, 2, tk), lambda bb, p, t, it_, jt_: (bb, p, 0, jt_[t]))],
        out_specs=pl.BlockSpec((1, tq, LANES), lambda bb, p, t, it_, jt_: (bb, it_[t], p)),
        scratch_shapes=[pltpu.VMEM((2, tq, 1), F32), pltpu.VMEM((2, tq, 1), F32),
                        pltpu.VMEM((2, tq, LANES), F32)])
    return pl.pallas_call(
        functools.partial(_fox_kernel, tq=tq, tk=tk, q_pos0=q_pos0, nk=nk),
        grid_spec=grid_spec,
        out_shape=jax.ShapeDtypeStruct(q.shape, BF16),
        compiler_params=pltpu.CompilerParams(dimension_semantics=("parallel", "parallel", "arbitrary"),
                                             vmem_limit_bytes=VMEM_LIMIT),
        name="fox_attn",
    )(it, jt, q, k, v, kb4)


def _dsa_kernel(dq_ref, iq_ref, w_ref, dk_ref, dv_ref, ik_ref, o_ref,
                key_ref, m_ref, l_ref, acc_ref, *, tq, tk, q_pos0, n_real, top_k, nk):
    i = pl.program_id(1)
    row0 = q_pos0 + i * tq
    n_adm = jnp.minimum(((row0 + tq - 1) // CHUNK + 1) * CHUNK, n_real)
    nt = (n_adm + tk - 1) // tk
    nsub = tk // LANES
    qchunk = jnp.right_shift(row0 + lax.broadcasted_iota(jnp.int32, (tq, 1), 0), CHUNK_SHIFT)

    def ktile(ref, j, lo, hi):
        return ref[0, pl.ds(pl.multiple_of(j * tk, tk), tk), lo:hi]

    def score_tile(j, carry):
        ikt = ktile(ik_ref, j, 0, LANES)
        sc = jnp.zeros((tq, tk), F32)
        for h in range(IDX_HEADS):
            qpair = iq_ref[0, :, (h // 2) * LANES:(h // 2 + 1) * LANES]
            qh = jnp.where(_half_mask(qpair.shape, h % 2), qpair, jnp.zeros_like(qpair))
            wh = w_ref[0, :, _MISC_IW + h:_MISC_IW + h + 1]
            sc = sc + wh * jnp.maximum(_nt_dot(qh, ikt), 0.0)
        bits = pltpu.bitcast(sc, jnp.int32)
        key = jnp.where(bits < 0, (bits ^ 0x7FFFFFFF) + 1, bits)
        col = j * tk + lax.broadcasted_iota(jnp.int32, (tq, tk), 1)
        adm = jnp.logical_and(jnp.right_shift(col, CHUNK_SHIFT) <= qchunk, col < n_real)
        key_ref[j] = jnp.where(adm, key, INT_MIN)
        return carry

    lax.fori_loop(0, nt, score_tile, 0)

    def count(pred):
        def body(j, acc):
            hit = jnp.where(pred(key_ref[j], j), 1.0, 0.0)
            for sblk in range(nsub):
                acc = acc + hit[:, sblk * LANES:(sblk + 1) * LANES]
            return acc
        acc = lax.fori_loop(0, nt, body, jnp.zeros((tq, LANES), F32))
        return jnp.sum(acc, axis=1, keepdims=True)

    def bit_step(it, carry):
        thr, cnt_thr = carry
        cand = thr + lax.shift_left(jnp.int32(1), 31 - it)
        cnt = count(lambda kt, j: kt >= cand)
        ok = cnt >= top_k
        return jnp.where(ok, cand, thr), jnp.where(ok, cnt, cnt_thr)

    thr0 = jnp.full((tq, 1), INT_MIN, jnp.int32)
    thr, cnt_thr = lax.fori_loop(0, 32, bit_step, (thr0, jnp.full((tq, 1), 2.0 ** 30, F32)))

    excess = jnp.logical_and(cnt_thr > top_k, thr > INT_MIN)

    @pl.when(jnp.max(jnp.where(excess, 1.0, 0.0)) > 0.0)
    def _():
        need = top_k - count(lambda kt, j: kt > thr)

        def col_of(j):
            return j * tk + lax.broadcasted_iota(jnp.int32, (tq, tk), 1)

        bound = jnp.zeros((tq, 1), jnp.int32)
        for bit in reversed(range(int(nk * tk - 1).bit_length())):
            cand = bound + (1 << bit)
            c = count(lambda kt, j: jnp.logical_and(kt == thr, col_of(j) < cand))
            bound = jnp.where(c < need, cand, bound)

        def demote(j, carry):
            kt = key_ref[j]
            drop = jnp.logical_and(jnp.logical_and(kt == thr, col_of(j) > bound), excess)
            key_ref[j] = jnp.where(drop, kt - 1, kt)
            return carry

        lax.fori_loop(0, nt, demote, 0)

    m_ref[...] = jnp.full_like(m_ref, NEG_INF)
    l_ref[...] = jnp.zeros_like(l_ref)
    acc_ref[...] = jnp.zeros_like(acc_ref)

    thr_sel = jnp.maximum(thr, INT_MIN + 1)

    def attend_tile(j, carry):
        sel = key_ref[j] >= thr_sel
        for g in range(DSA_KV_HEADS):
            kk = ktile(dk_ref, j, g * LANES, (g + 1) * LANES)
            vv = ktile(dv_ref, j, g * LANES, (g + 1) * LANES)
            for h in range(g * 4, g * 4 + 4):
                qpair = dq_ref[0, :, (h // 2) * LANES:(h // 2 + 1) * LANES]
                qh = jnp.where(_half_mask(qpair.shape, h % 2), qpair, jnp.zeros_like(qpair))
                s = jnp.where(sel, _nt_dot(qh, kk), NEG_INF)
                m_prev = m_ref[h]
                m_new = jnp.maximum(m_prev, jnp.max(s, axis=1, keepdims=True))
                m_safe = jnp.where(m_new == NEG_INF, 0.0, m_new)
                alpha = jnp.exp(m_prev - m_safe)
                p = jnp.exp(s - m_safe)
                l_ref[h] = alpha * l_ref[h] + jnp.sum(p, axis=1, keepdims=True)
                acc_ref[h] = alpha * acc_ref[h] + jnp.dot(p.astype(BF16), vv, preferred_element_type=F32)
                m_ref[h] = m_new
        return carry

    lax.fori_loop(0, nt, attend_tile, 0)

    for c in range(DSA_HEADS // 2):
        o0 = acc_ref[2 * c] / l_ref[2 * c]
        o1 = acc_ref[2 * c + 1] / l_ref[2 * c + 1]
        o_ref[0, :, c * LANES:(c + 1) * LANES] = jnp.where(_half_mask(o0.shape, 0), o0, o1).astype(o_ref.dtype)


def _dsa_attn(dq, iq, misc, dk2, dv2, ik2, *, q_pos0, n_real, top_k, tq, tk):
    b, tq_all, _ = dq.shape
    tk_all = dk2.shape[1]
    assert tq_all % tq == 0 and tk_all % tk == 0 and tk % LANES == 0
    nk = tk_all // tk
    qspec = lambda w: pl.BlockSpec((1, tq, w), lambda bb, i: (bb, i, 0))
    kspec = lambda w: pl.BlockSpec((1, tk_all, w), lambda bb, i: (bb, 0, 0))
    return pl.pallas_call(
        functools.partial(_dsa_kernel, tq=tq, tk=tk, q_pos0=q_pos0, n_real=n_real, top_k=top_k, nk=nk),
        grid=(b, tq_all // tq),
        in_specs=[qspec(DSA_WIDTH), qspec(IDX_HEADS * HEAD_DIM), qspec(LANES),
                  kspec(2 * LANES), kspec(2 * LANES), kspec(LANES)],
        out_specs=qspec(DSA_WIDTH),
        out_shape=jax.ShapeDtypeStruct(dq.shape, BF16),
        scratch_shapes=[pltpu.VMEM((nk, tq, tk), jnp.int32),
                        pltpu.VMEM((DSA_HEADS, tq, 1), F32), pltpu.VMEM((DSA_HEADS, tq, 1), F32),
                        pltpu.VMEM((DSA_HEADS, tq, LANES), F32)],
        compiler_params=pltpu.CompilerParams(dimension_semantics=("parallel", "arbitrary"),
                                             vmem_limit_bytes=VMEM_LIMIT),
        name="dsa_attn",
    )(dq, iq, misc, dk2, dv2, ik2)


def _layer_norm(x, g, b):
    mu = jnp.mean(x, axis=-1, keepdims=True)
    xc = x - mu
    var = jnp.mean(xc * xc, axis=-1, keepdims=True)
    return xc * lax.rsqrt(var + LN_EPS) * g + b


def _post_kernel(x_ref, fox_ref, dsa_ref, wo_ref, g1_ref, b1_ref, wg_ref, wu_ref, wd_ref, g2_ref, b2_ref,
                 o_ref, *, alpha, ff_chunk):
    mix = (jnp.dot(fox_ref[...], wo_ref[:FOX_WIDTH, :], preferred_element_type=F32)
           + jnp.dot(dsa_ref[...], wo_ref[FOX_WIDTH:, :], preferred_element_type=F32))
    h = _layer_norm(alpha * x_ref[...] + mix, g1_ref[...], b1_ref[...])
    hb = h.astype(BF16)
    f = jnp.zeros_like(h)
    for c in range(wg_ref.shape[1] // ff_chunk):
        lo, hi = c * ff_chunk, (c + 1) * ff_chunk
        gate = jnp.dot(hb, wg_ref[:, lo:hi], preferred_element_type=F32)
        up = jnp.dot(hb, wu_ref[:, lo:hi], preferred_element_type=F32)
        act = (gate * jax.nn.sigmoid(gate) * up).astype(BF16)
        f = f + jnp.dot(act, wd_ref[lo:hi, :], preferred_element_type=F32)
    o_ref[...] = _layer_norm(alpha * h + f, g2_ref[...], b2_ref[...])


def _post(x2, fox_o, dsa_o, w_out, ln1_g, ln1_b, w_gate, w_up, w_down, ln2_g, ln2_b, *, alpha, tm, ff_chunk=256):
    n, d = x2.shape
    d_ff = w_gate.shape[1]
    assert n % tm == 0 and d_ff % ff_chunk == 0
    row = lambda t: (t, 0)
    const = lambda t: (0, 0)
    wspec = lambda shape: pl.BlockSpec(shape, const, pipeline_mode=pl.Buffered(1))
    vec = lambda a: a.astype(F32)[None, :]
    return pl.pallas_call(
        functools.partial(_post_kernel, alpha=alpha, ff_chunk=ff_chunk),
        grid=(n // tm,),
        in_specs=[pl.BlockSpec((tm, d), row), pl.BlockSpec((tm, FOX_WIDTH), row), pl.BlockSpec((tm, DSA_WIDTH), row),
                  wspec(w_out.shape), wspec((1, d)), wspec((1, d)),
                  wspec(w_gate.shape), wspec(w_up.shape), wspec(w_down.shape), wspec((1, d)), wspec((1, d))],
        out_specs=pl.BlockSpec((tm, d), row),
        out_shape=jax.ShapeDtypeStruct((n, d), F32),
        compiler_params=pltpu.CompilerParams(dimension_semantics=("parallel",), vmem_limit_bytes=VMEM_LIMIT),
        name="post",
    )(x2, fox_o, dsa_o, w_out.astype(BF16), vec(ln1_g), vec(ln1_b),
      w_gate.astype(BF16), w_up.astype(BF16), w_down.astype(BF16), vec(ln2_g), vec(ln2_b))


def _round_up(n, m):
    return -(-n // m) * m


def _pad_rows(a, rows):
    return jnp.pad(a, ((0, 0), (0, rows - a.shape[1]), (0, 0)))


def _head_major(c):
    return jnp.transpose(c[:, :, :FOX_HEADS], (0, 2, 1))


def _mixer_layer(y, past, weights, *, alpha):
    (w_packed, bias, w_out, ln1_g, ln1_b, w_gate, w_up, w_down, ln2_g, ln2_b) = weights
    b, t, d = y.shape
    p_len = 0 if past is None else past[0].shape[1]
    tables = _rope_lane_tables(p_len + jnp.arange(t))
    tm = min(512, t)
    (fq, fk, fkb, fv, fvb, dq, dk, dv, dk2, dv2, iq, ik, ik2, misc) = _project(
        y.reshape(b * t, d), w_packed, bias, tables, t, tm)
    r3 = lambda a: a.reshape(b, t, a.shape[-1])
    fq, fkb, fvb, dq, dk2, dv2, iq, ik2, misc3 = map(r3, (fq, fkb, fvb, dq, dk2, dv2, iq, ik2, misc))

    t_pad = _round_up(t, LANES)
    c_new = _cumsum(_pad_rows(misc3, t_pad))[:, :t]
    n_keys = p_len + t
    top_k = min(IDX_TOPK_MAX, n_keys // 4)
    if past is None:
        k_all, v_all, kb = fkb, fvb, -_head_major(c_new)
        dk_all, dv_all, ik_all = dk2, dv2, ik2
        tq = min(512, t)
        tk_fox = tq
        tq_dsa = tk_dsa = min(256, t)
    else:
        cf_k, cf_v, cf_logf, cd_k, cd_v, ci_k = past
        n_pad = _round_up(n_keys, LANES)
        cat = lambda old, new: _pad_rows(jnp.concatenate([old.astype(BF16), new], axis=1), n_pad)
        dup = lambda a: jnp.repeat(a, 2, axis=2).reshape(b, p_len, 2 * LANES)
        k_all = cat(cf_k.reshape(b, p_len, FOX_WIDTH), fkb)
        v_all = cat(cf_v.reshape(b, p_len, FOX_WIDTH), fvb)
        dk_all, dv_all = cat(dup(cd_k), dk2), cat(dup(cd_v), dv2)
        ik_all = cat(jnp.concatenate([ci_k, ci_k], axis=-1), ik2)
        lf_past = jnp.pad(cf_logf.astype(F32), ((0, 0), (0, 0), (0, LANES - FOX_HEADS)))
        d_past = _cumsum(lf_past, reverse=True, inclusive=False)
        kb = jnp.concatenate([_head_major(d_past), -_head_major(c_new)], axis=2)
        kb = jnp.pad(kb, ((0, 0), (0, 0), (0, n_pad - n_keys)))
        tq = t
        tk_fox = next(c for c in (1408, 1024, 512, 384, 256, 128) if n_pad % c == 0)
        tq_dsa, tk_dsa = t, LANES
    fox_o = _fox_attn(fq, k_all, v_all, kb, q_pos0=p_len, tq=tq, tk=tk_fox)
    dsa_o = _dsa_attn(dq, iq, misc3, dk_all, dv_all, ik_all, q_pos0=p_len, n_real=n_keys, top_k=top_k,
                      tq=tq_dsa, tk=tk_dsa)
    y_out = _post(y.reshape(b * t, d), fox_o.reshape(b * t, -1), dsa_o.reshape(b * t, -1),
                  w_out, ln1_g, ln1_b, w_gate, w_up, w_down, ln2_g, ln2_b, alpha=alpha, tm=min(512, b * t))
    states = (fk.reshape(b, t, FOX_HEADS, HEAD_DIM), fv.reshape(b, t, FOX_HEADS, HEAD_DIM),
              misc3[:, :, :FOX_HEADS],
              dk.reshape(b, t, DSA_KV_HEADS, HEAD_DIM), dv.reshape(b, t, DSA_KV_HEADS, HEAD_DIM),
              ik.reshape(b, t, HEAD_DIM))
    return y_out.reshape(b, t, d), states


def kernel(x_prompt, x_sample, cache_fox_k, cache_fox_v, cache_fox_logf, cache_dsa_k, cache_dsa_v, cache_idx_k,
           w_in, b_forget, w_out, ln1_g, ln1_b, w_gate, w_up, w_down, ln2_g, ln2_b):
    depth = w_in.shape[0]
    alpha = (2 * depth) ** 0.25
    yp, ys = x_prompt, x_sample
    p_states, s_states = [], []
    for l in range(depth):
        w_packed, bias = _pack_w_in(w_in[l], b_forget[l])
        weights = (w_packed, bias, w_out[l], ln1_g[l], ln1_b[l], w_gate[l], w_up[l], w_down[l], ln2_g[l], ln2_b[l])
        yp, st = _mixer_layer(yp, None, weights, alpha=alpha)
        p_states.append(st)
        past = (cache_fox_k[l], cache_fox_v[l], cache_fox_logf[l], cache_dsa_k[l], cache_dsa_v[l], cache_idx_k[l])
        ys, st = _mixer_layer(ys, past, weights, alpha=alpha)
        s_states.append(st)
    stack = lambda sts: tuple(jnp.stack([s[n] for s in sts], axis=0) for n in range(6))
    return (yp, ys) + stack(p_states) + stack(s_states)
```

```python
import functools

import numpy as np
import jax
import jax.numpy as jnp
from jax import lax
from jax.experimental import pallas as pl
from jax.experimental.pallas import tpu as pltpu

HEAD_DIM = 64
FOX_HEADS = 8
DSA_HEADS = 8
DSA_KV_HEADS = 2
IDX_HEADS = 4
CHUNK = 64
CHUNK_SHIFT = 6
IDX_TOPK_MAX = 256
ROPE_THETA = 500000.0
ROT_DIM = HEAD_DIM // 4
LN_EPS = 1e-5
LOG2E = 1.4426950408889634

LANES = 128
SUBLANES = 8
FOX_WIDTH = FOX_HEADS * HEAD_DIM
DSA_WIDTH = DSA_HEADS * HEAD_DIM
INT_MIN = -(2 ** 31)
NEG_INF = float("-inf")
F32 = jnp.float32
BF16 = jnp.bfloat16
VMEM_LIMIT = 56 * 1024 * 1024

_C_FQ, _C_FK, _C_FV, _C_DQ = 0, 512, 1024, 1536
_C_DK, _C_DV, _C_IQ, _C_IK, _C_MISC, _C_END = 2048, 2176, 2304, 2560, 2688, 2816
_MISC_IW = 8


def _nt_dot(a, b):
    return lax.dot_general(a, b, (((1,), (1,)), ((), ())), preferred_element_type=F32)


def _half_mask(shape, hh):
    lane = lax.broadcasted_iota(jnp.int32, shape, len(shape) - 1)
    return (lane < HEAD_DIM) if hh == 0 else (lane >= HEAD_DIM)


def _project_kernel(x_ref, w_ref, bias_ref, c_ref, sa_ref, sb_ref,
                    fq_ref, fk_ref, fkb_ref, fv_ref, fvb_ref, dq_ref, dk_ref, dv_ref,
                    dk2_ref, dvb_ref, iq_ref, ik_ref, ik2_ref, misc_ref):
    xb = x_ref[...].astype(BF16)
    cos, sa, sb = c_ref[...], sa_ref[...], sb_ref[...]

    def mm(lo, hi):
        return jnp.dot(xb, w_ref[:, lo:hi], preferred_element_type=F32)

    def rope(yb):
        return yb * cos + pltpu.roll(yb, LANES - ROT_DIM // 2, 1) * sa + pltpu.roll(yb, ROT_DIM // 2, 1) * sb

    def blocks(y):
        return [y[:, b * LANES:(b + 1) * LANES] for b in range(y.shape[1] // LANES)]

    fq_ref[...] = mm(_C_FQ, _C_FK).astype(BF16)
    y = mm(_C_FK, _C_FV)
    fk_ref[...] = y
    fkb_ref[...] = y.astype(BF16)
    y = mm(_C_FV, _C_DQ)
    fv_ref[...] = y
    fvb_ref[...] = y.astype(BF16)
    for b, yb in enumerate(blocks(mm(_C_DQ, _C_DK))):
        dq_ref[:, b * LANES:(b + 1) * LANES] = rope(yb).astype(BF16)

    half0 = _half_mask((x_ref.shape[0], LANES), 0)
    kv = mm(_C_DK, _C_IQ)
    k = rope(kv[:, :LANES])
    v = kv[:, LANES:]
    dk_ref[...] = k
    dv_ref[...] = v
    dvb_ref[...] = v.astype(BF16)
    sw = pltpu.roll(k, HEAD_DIM, 1)
    dk2_ref[:, :LANES] = jnp.where(half0, k, sw).astype(BF16)
    dk2_ref[:, LANES:] = jnp.where(half0, sw, k).astype(BF16)

    for b, yb in enumerate(blocks(mm(_C_IQ, _C_IK))):
        iq_ref[:, b * LANES:(b + 1) * LANES] = rope(yb).astype(BF16)
    ik = rope(mm(_C_IK, _C_MISC))
    ik_ref[...] = ik[:, :HEAD_DIM]
    ik2_ref[...] = ik.astype(BF16)

    z = mm(_C_MISC, _C_END)
    zf = z + bias_ref[...]
    logf = jnp.minimum(zf, 0.0) - jnp.log1p(jnp.exp(-jnp.abs(zf)))
    lane = lax.broadcasted_iota(jnp.int32, z.shape, 1)
    misc_ref[...] = jnp.where(lane < _MISC_IW, logf, z)


def _pack_w_in(w, b_forget):
    d = w.shape[0]
    sizes = (FOX_WIDTH, FOX_WIDTH, FOX_WIDTH, FOX_HEADS, DSA_WIDTH, DSA_KV_HEADS * HEAD_DIM,
             DSA_KV_HEADS * HEAD_DIM, IDX_HEADS * HEAD_DIM, HEAD_DIM, IDX_HEADS)
    offs = np.concatenate([[0], np.cumsum(sizes)])
    fq, fk, fv, ff, dq, dk, dv, iq, ik, iw = [w[:, offs[i]:offs[i + 1]] for i in range(len(sizes))]
    scale = HEAD_DIM ** -0.5 * LOG2E
    misc = jnp.concatenate([ff, iw, jnp.zeros((d, LANES - FOX_HEADS - IDX_HEADS), w.dtype)], axis=1)
    packed = jnp.concatenate([fq * scale, fk, fv, dq * scale, dk, dv, iq, ik, ik, misc], axis=1)
    bias = jnp.concatenate([b_forget.astype(F32), jnp.zeros((LANES - FOX_HEADS,), F32)])[None, :]
    return packed.astype(BF16), bias


def _rope_lane_tables(positions):
    half = ROT_DIM // 2
    inv_freq = ROPE_THETA ** (-jnp.arange(half, dtype=F32) * 2.0 / ROT_DIM)
    ang = positions.astype(F32)[:, None] * inv_freq[None, :]
    cos, sin = jnp.cos(ang), jnp.sin(ang)
    n = positions.shape[0]
    rest = HEAD_DIM - ROT_DIM
    c = jnp.concatenate([cos, cos, jnp.ones((n, rest), F32)], axis=1)
    sa = jnp.concatenate([-sin, jnp.zeros((n, half + rest), F32)], axis=1)
    sb = jnp.concatenate([jnp.zeros((n, half), F32), sin, jnp.zeros((n, rest), F32)], axis=1)
    rep = LANES // HEAD_DIM
    return tuple(jnp.tile(t, (1, rep)) for t in (c, sa, sb))


def _project(x2, w_packed, bias, tables, seq, tm):
    n, d = x2.shape
    assert seq % tm == 0 and n % seq == 0
    tiles_per_seq = seq // tm
    row = lambda t: (t, 0)
    pos = lambda t: (t % tiles_per_seq, 0)
    const = lambda t: (0, 0)

    def out(width, dtype):
        return jax.ShapeDtypeStruct((n, width), dtype), pl.BlockSpec((tm, width), row)

    outs = [out(FOX_WIDTH, BF16),
            out(FOX_WIDTH, F32), out(FOX_WIDTH, BF16),
            out(FOX_WIDTH, F32), out(FOX_WIDTH, BF16),
            out(DSA_WIDTH, BF16),
            out(LANES, F32), out(LANES, F32),
            out(2 * LANES, BF16), out(LANES, BF16),
            out(IDX_HEADS * HEAD_DIM, BF16),
            out(HEAD_DIM, F32), out(LANES, BF16),
            out(LANES, F32)]
    return pl.pallas_call(
        _project_kernel,
        grid=(n // tm,),
        in_specs=[pl.BlockSpec((tm, d), row),
                  pl.BlockSpec((d, _C_END), const, pipeline_mode=pl.Buffered(1)),
                  pl.BlockSpec((1, LANES), const),
                  pl.BlockSpec((tm, LANES), pos), pl.BlockSpec((tm, LANES), pos), pl.BlockSpec((tm, LANES), pos)],
        out_specs=[o[1] for o in outs],
        out_shape=[o[0] for o in outs],
        compiler_params=pltpu.CompilerParams(dimension_semantics=("parallel",), vmem_limit_bytes=VMEM_LIMIT),
        name="project",
    )(x2, w_packed, bias, *tables)


def _cumsum_kernel(x_ref, o_ref, carry_ref, *, reverse, inclusive, scale):
    @pl.when(pl.program_id(1) == 0)
    def _():
        carry_ref[...] = jnp.zeros_like(carry_ref)

    x = x_ref[0]
    tm = x.shape[0]
    hi = x.astype(BF16)
    r1 = x - hi.astype(F32)
    mid = r1.astype(BF16)
    lo = (r1 - mid.astype(F32)).astype(BF16)
    row = lax.broadcasted_iota(jnp.int32, (tm, tm), 0)
    col = lax.broadcasted_iota(jnp.int32, (tm, tm), 1)
    if reverse:
        keep = (col >= row) if inclusive else (col > row)
    else:
        keep = (col <= row) if inclusive else (col < row)
    tri = jnp.where(keep, 1.0, 0.0).astype(BF16)
    cs = (jnp.dot(tri, hi, preferred_element_type=F32) + jnp.dot(tri, mid, preferred_element_type=F32)
          + jnp.dot(tri, lo, preferred_element_type=F32))
    carry = carry_ref[0:1, :]
    o_ref[0] = scale * (cs + carry)
    carry_ref[0:1, :] = carry + jnp.sum(x, axis=0, keepdims=True)


def _cumsum(x, *, reverse=False, inclusive=True, scale=1.0, tm=512):
    b, s, w = x.shape
    tm = min(tm, s)
    assert s % tm == 0 and w == LANES
    nt = s // tm
    idx = (lambda i, t: (i, nt - 1 - t, 0)) if reverse else (lambda i, t: (i, t, 0))
    return pl.pallas_call(
        functools.partial(_cumsum_kernel, reverse=reverse, inclusive=inclusive, scale=scale),
        grid=(b, nt),
        in_specs=[pl.BlockSpec((1, tm, w), idx)],
        out_specs=pl.BlockSpec((1, tm, w), idx),
        out_shape=jax.ShapeDtypeStruct(x.shape, F32),
        scratch_shapes=[pltpu.VMEM((SUBLANES, w), F32)],
        compiler_params=pltpu.CompilerParams(dimension_semantics=("parallel", "arbitrary")),
        name="cumsum",
    )(x)


BIAS_PIECES = 3


def _bias_lane0(h):
    return HEAD_DIM if h % 2 == 0 else 0


def _mask_heads(q_ref, qm_ref, n_heads, bias_ones=False):
    for h in range(n_heads):
        qpair = q_ref[0, :, (h // 2) * LANES:(h // 2 + 1) * LANES]
        other = jnp.zeros_like(qpair)
        if bias_ones:
            lane = lax.broadcasted_iota(jnp.int32, qpair.shape, 1)
            in_bias = jnp.logical_and(lane >= _bias_lane0(h), lane < _bias_lane0(h) + BIAS_PIECES)
            other = jnp.where(in_bias, 1.0, 0.0).astype(qpair.dtype)
        qm_ref[h] = jnp.where(_half_mask(qpair.shape, h % 2), qpair, other)


def _flash_scratch(n_heads, tq, tk):
    return [pltpu.VMEM((2, n_heads, tk, tq), F32), pltpu.VMEM((2, n_heads, SUBLANES, tq), F32),
            pltpu.VMEM((n_heads, 1, tq), F32), pltpu.VMEM((n_heads, 1, tq), F32),
            pltpu.VMEM((n_heads, HEAD_DIM, tq), F32)]


def _init_softmax(m_ref, l_ref, acc_ref):
    m_ref[...] = jnp.full_like(m_ref, NEG_INF)
    l_ref[...] = jnp.zeros_like(l_ref)
    acc_ref[...] = jnp.zeros_like(acc_ref)


def _flash_loop(segments, logits0, values, tk, s_ref, st_ref, m_ref, l_ref, acc_ref):
    ones = jnp.ones((2 * SUBLANES, tk), BF16)

    def by_parity(j, fn):
        for slot in range(2):
            pl.when(lax.rem(j, 2) == slot)(functools.partial(fn, slot))

    def stage_a(j, logits, slot):
        for h, s in enumerate(logits(j)):
            s_ref[slot, h] = s
            m_prev = m_ref[h]
            m_new = jnp.maximum(m_prev, jnp.max(s, axis=0, keepdims=True))
            m_use = jnp.where(m_new == NEG_INF, 0.0, m_new)
            st_ref[slot, h, 0:1, :] = m_use
            st_ref[slot, h, 1:2, :] = jnp.exp2(m_prev - m_use)
            m_ref[h] = m_new

    def stage_b(j, slot):
        for h in range(m_ref.shape[0]):
            p = jnp.exp2(s_ref[slot, h] - st_ref[slot, h, 0:1, :]).astype(BF16)
            alpha = st_ref[slot, h, 1:2, :]
            l_ref[h] = alpha * l_ref[h] + jnp.dot(ones, p, preferred_element_type=F32)[0:1]
            acc_ref[h] = alpha * acc_ref[h] + jnp.dot(values(j, h), p, preferred_element_type=F32)

    stage_a(0, logits0, 0)
    start = 1
    for end, logits in segments:
        def both(slot, j, logits):
            stage_a(j, logits, slot)
            stage_b(j - 1, 1 - slot)

        def body(j, carry, logits=logits):
            by_parity(j, functools.partial(both, j=j, logits=logits))
            return carry

        lax.fori_loop(start, end, body, 0)
        start = jnp.maximum(start, end)
    by_parity(start - 1, lambda slot: stage_b(start - 1, slot))


def _store_heads(o_ref, l_ref, acc_ref, n_heads):
    for c in range(n_heads // 2):
        ot = jnp.concatenate([acc_ref[2 * c] / l_ref[2 * c], acc_ref[2 * c + 1] / l_ref[2 * c + 1]], axis=0)
        o_ref[0, :, c * LANES:(c + 1) * LANES] = ot.T.astype(o_ref.dtype)


def _key_major(v, tk):
    b, t, c = v.shape
    return jnp.transpose(v.reshape(b, t // tk, tk, c), (0, 1, 3, 2))


def _fox_keys_kernel(k_ref, kb_ref, o_ref):
    kb = kb_ref[0]
    hi = kb.astype(BF16).astype(F32)
    r1 = kb - hi
    mid = r1.astype(BF16).astype(F32)
    pieces = (hi, mid, r1 - mid)
    lane = lax.broadcasted_iota(jnp.int32, kb.shape, 1)
    for h in range(FOX_HEADS):
        bias = jnp.zeros_like(kb)
        for n, piece in enumerate(pieces):
            dst = _bias_lane0(h) + n
            bias = jnp.where(lane == dst, pltpu.roll(piece, (dst - h) % LANES, 1), bias)
        kpair = k_ref[0, :, (h // 2) * LANES:(h // 2 + 1) * LANES]
        o_ref[0, :, h * LANES:(h + 1) * LANES] = jnp.where(_half_mask(kpair.shape, h % 2), kpair, bias.astype(BF16))


def _fox_keys(k, kb, tm=512):
    b, t, _ = k.shape
    tm = _largest_tile(t, (tm, 384, 256, 128))
    spec = lambda w: pl.BlockSpec((1, tm, w), lambda bb, i: (bb, i, 0))
    return pl.pallas_call(
        _fox_keys_kernel,
        grid=(b, t // tm),
        in_specs=[spec(FOX_WIDTH), spec(LANES)],
        out_specs=spec(FOX_HEADS * LANES),
        out_shape=jax.ShapeDtypeStruct((b, t, FOX_HEADS * LANES), BF16),
        compiler_params=pltpu.CompilerParams(dimension_semantics=("parallel", "parallel")),
        name="fox_keys",
    )(k, kb)


def _fox_kernel(q_ref, k_ref, vt_ref, o_ref, qm_ref, s_ref, st_ref, m_ref, l_ref, acc_ref, *, tq, tk, q_pos0, nk):
    row0 = q_pos0 + pl.program_id(1) * tq
    n_full = jnp.minimum(nk, (row0 + 1) // tk)
    j_end = jnp.minimum(nk, (row0 + tq - 1) // tk + 1)
    _mask_heads(q_ref, qm_ref, FOX_HEADS, bias_ones=True)
    _init_softmax(m_ref, l_ref, acc_ref)

    def logits(j, masked):
        off = pl.multiple_of(j * tk, tk)
        if masked:
            key_pos = j * tk + lax.broadcasted_iota(jnp.int32, (tk, tq), 0)
            visible = key_pos <= row0 + lax.broadcasted_iota(jnp.int32, (tk, tq), 1)
        for h in range(FOX_HEADS):
            s = _nt_dot(k_ref[0, pl.ds(off, tk), h * LANES:(h + 1) * LANES], qm_ref[h])
            yield jnp.where(visible, s, NEG_INF) if masked else s

    def values(j, h):
        return vt_ref[0, j, h * HEAD_DIM:(h + 1) * HEAD_DIM, :]

    masked = functools.partial(logits, masked=True)
    _flash_loop([(n_full, functools.partial(logits, masked=False)), (j_end, masked)], masked, values, tk,
                s_ref, st_ref, m_ref, l_ref, acc_ref)
    _store_heads(o_ref, l_ref, acc_ref, FOX_HEADS)


def _fox_attn(q, k, v, kb, *, q_pos0, tq, tk):
    b, tq_all, _ = q.shape
    tk_all = k.shape[1]
    assert tq_all % tq == 0 and tk_all % tk == 0 and tq % LANES == 0
    nk = tk_all // tk
    vt = _key_major(v, tk)
    k_aug = _fox_keys(k, kb)
    return pl.pallas_call(
        functools.partial(_fox_kernel, tq=tq, tk=tk, q_pos0=q_pos0, nk=nk),
        grid=(b, tq_all // tq),
        in_specs=[pl.BlockSpec((1, tq, FOX_WIDTH), lambda bb, i: (bb, i, 0)),
                  pl.BlockSpec((1, tk_all, FOX_HEADS * LANES), lambda bb, i: (bb, 0, 0),
                               pipeline_mode=pl.Buffered(1)),
                  pl.BlockSpec((1, nk, FOX_WIDTH, tk), lambda bb, i: (bb, 0, 0, 0), pipeline_mode=pl.Buffered(1))],
        out_specs=pl.BlockSpec((1, tq, FOX_WIDTH), lambda bb, i: (bb, i, 0)),
        out_shape=jax.ShapeDtypeStruct(q.shape, BF16),
        scratch_shapes=[pltpu.VMEM((FOX_HEADS, tq, LANES), BF16)] + _flash_scratch(FOX_HEADS, tq, tk),
        compiler_params=pltpu.CompilerParams(dimension_semantics=("parallel", "arbitrary"),
                                             vmem_limit_bytes=VMEM_LIMIT),
        name="fox_attn",
    )(q, k_aug, vt)


def _dsa_kernel(dq_ref, iq_ref, w_ref, dk_ref, dvt_ref, ik_ref, o_ref,
                qm_ref, iqm_ref, key_ref, drop_ref, s_ref, st_ref, m_ref, l_ref, acc_ref,
                *, tq, tk, q_pos0, n_real, top_k, nk):
    row0 = q_pos0 + pl.program_id(1) * tq
    q_pos = row0 + lax.broadcasted_iota(jnp.int32, (1, tq), 1)
    lim = jnp.minimum(lax.shift_left(jnp.right_shift(q_pos, CHUNK_SHIFT) + 1, CHUNK_SHIFT), n_real)
    lim_first = jnp.minimum((row0 // CHUNK + 1) * CHUNK, n_real)
    lim_last = jnp.minimum(((row0 + tq - 1) // CHUNK + 1) * CHUNK, n_real)
    n_full = lim_first // tk
    nt = (lim_last + tk - 1) // tk
    _mask_heads(dq_ref, qm_ref, DSA_HEADS)
    _mask_heads(iq_ref, iqm_ref, IDX_HEADS)

    def ktile(ref, j, lo, hi):
        return ref[0, pl.ds(pl.multiple_of(j * tk, tk), tk), lo:hi]

    def key_pos(j):
        return j * tk + lax.broadcasted_iota(jnp.int32, (tk, tq), 0)

    def score_tile(j, edge):
        ikt = ktile(ik_ref, j, 0, LANES)
        sc = jnp.zeros((tk, tq), F32)
        for h in range(IDX_HEADS):
            sc = sc + w_ref[0, h:h + 1, :] * jnp.maximum(_nt_dot(ikt, iqm_ref[h]), 0.0)
        bits = pltpu.bitcast(sc, jnp.int32)
        key = jnp.where(bits < 0, (bits ^ 0x7FFFFFFF) + 1, bits)
        if edge:
            key = jnp.where(key_pos(j) < lim, key, INT_MIN)
        key_ref[j] = key

    def score_full(j, carry):
        score_tile(j, False)
        return carry

    def score_edge(j, carry):
        score_tile(j, True)
        return carry

    lax.fori_loop(0, n_full, score_full, 0)
    lax.fori_loop(n_full, nt, score_edge, 0)

    def count(pred):
        def body(j, acc):
            hit = jnp.where(pred(key_ref[j], j), 1.0, 0.0)
            return acc + jnp.sum(hit.reshape(tk // SUBLANES, SUBLANES, tq), axis=0)
        acc = lax.fori_loop(0, nt, body, jnp.zeros((SUBLANES, tq), F32))
        return jnp.sum(acc, axis=0, keepdims=True)

    def bit_step(it, carry):
        thr, cnt_thr = carry
        cand = thr + lax.shift_left(jnp.int32(1), 31 - it)
        cnt = count(lambda kt, j: kt >= cand)
        ok = cnt >= top_k
        return jnp.where(ok, cand, thr), jnp.where(ok, cnt, cnt_thr)

    thr0 = jnp.full((1, tq), INT_MIN, jnp.int32)
    thr, cnt_thr = lax.fori_loop(0, 32, bit_step, (thr0, jnp.full((1, tq), 2.0 ** 30, F32)))

    excess = jnp.logical_and(cnt_thr > top_k, thr > INT_MIN)

    @pl.when(jnp.max(jnp.where(excess, 1.0, 0.0)) > 0.0)
    def _():
        need = top_k - count(lambda kt, j: kt > thr)
        bound = jnp.zeros((1, tq), jnp.int32)
        for bit in reversed(range(int(nk * tk - 1).bit_length())):
            cand = bound + (1 << bit)
            c = count(lambda kt, j: jnp.logical_and(kt == thr, key_pos(j) < cand))
            bound = jnp.where(c < need, cand, bound)

        def demote(j, carry):
            kt = key_ref[j]
            drop = jnp.logical_and(jnp.logical_and(kt == thr, key_pos(j) > bound), excess)
            key_ref[j] = jnp.where(drop, kt - 1, kt)
            return carry

        lax.fori_loop(0, nt, demote, 0)

    thr_sel = jnp.maximum(thr, INT_MIN + 1)
    _init_softmax(m_ref, l_ref, acc_ref)
    kv_head = lambda h: h // (DSA_HEADS // DSA_KV_HEADS)

    def logits(j):
        drop_ref[...] = jnp.where(key_ref[j] >= thr_sel, 0.0, NEG_INF)
        for h in range(DSA_HEADS):
            g = kv_head(h)
            yield _nt_dot(ktile(dk_ref, j, g * LANES, (g + 1) * LANES), qm_ref[h]) + drop_ref[...]

    def values(j, h):
        return dvt_ref[0, j, kv_head(h) * HEAD_DIM:(kv_head(h) + 1) * HEAD_DIM, :]

    _flash_loop([(nt, logits)], logits, values, tk, s_ref, st_ref, m_ref, l_ref, acc_ref)
    _store_heads(o_ref, l_ref, acc_ref, DSA_HEADS)


def _dsa_attn(dq, iq, w_t, dk2, dv, ik2, *, q_pos0, n_real, top_k, tq, tk):
    b, tq_all, _ = dq.shape
    tk_all = dk2.shape[1]
    assert tq_all % tq == 0 and tk_all % tk == 0 and tq % LANES == 0 and tk % SUBLANES == 0
    nk = tk_all // tk
    dvt = _key_major(dv, tk)
    qspec = lambda w: pl.BlockSpec((1, tq, w), lambda bb, i: (bb, i, 0))
    kspec = lambda w: pl.BlockSpec((1, tk_all, w), lambda bb, i: (bb, 0, 0))
    return pl.pallas_call(
        functools.partial(_dsa_kernel, tq=tq, tk=tk, q_pos0=q_pos0, n_real=n_real, top_k=top_k, nk=nk),
        grid=(b, tq_all // tq),
        in_specs=[qspec(DSA_WIDTH), qspec(IDX_HEADS * HEAD_DIM),
                  pl.BlockSpec((1, SUBLANES, tq), lambda bb, i: (bb, 0, i)),
                  kspec(2 * LANES),
                  pl.BlockSpec((1, nk, LANES, tk), lambda bb, i: (bb, 0, 0, 0)),
                  kspec(LANES)],
        out_specs=qspec(DSA_WIDTH),
        out_shape=jax.ShapeDtypeStruct(dq.shape, BF16),
        scratch_shapes=[pltpu.VMEM((DSA_HEADS, tq, LANES), BF16), pltpu.VMEM((IDX_HEADS, tq, LANES), BF16),
                        pltpu.VMEM((nk, tk, tq), jnp.int32), pltpu.VMEM((tk, tq), F32)]
                       + _flash_scratch(DSA_HEADS, tq, tk),
        compiler_params=pltpu.CompilerParams(dimension_semantics=("parallel", "arbitrary"),
                                             vmem_limit_bytes=VMEM_LIMIT),
        name="dsa_attn",
    )(dq, iq, w_t, dk2, dvt, ik2)


def _layer_norm(x, g, b):
    mu = jnp.mean(x, axis=-1, keepdims=True)
    xc = x - mu
    var = jnp.mean(xc * xc, axis=-1, keepdims=True)
    return xc * lax.rsqrt(var + LN_EPS) * g + b


def _post_kernel(x_ref, fox_ref, dsa_ref, wo_ref, g1_ref, b1_ref, wg_ref, wu_ref, wd_ref, g2_ref, b2_ref,
                 o_ref, *, alpha, ff_chunk):
    mix = (jnp.dot(fox_ref[...], wo_ref[:FOX_WIDTH, :], preferred_element_type=F32)
           + jnp.dot(dsa_ref[...], wo_ref[FOX_WIDTH:, :], preferred_element_type=F32))
    h = _layer_norm(alpha * x_ref[...] + mix, g1_ref[...], b1_ref[...])
    hb = h.astype(BF16)
    f = jnp.zeros_like(h)
    for c in range(wg_ref.shape[1] // ff_chunk):
        lo, hi = c * ff_chunk, (c + 1) * ff_chunk
        gate = jnp.dot(hb, wg_ref[:, lo:hi], preferred_element_type=F32)
        up = jnp.dot(hb, wu_ref[:, lo:hi], preferred_element_type=F32)
        act = (gate * jax.nn.sigmoid(gate) * up).astype(BF16)
        f = f + jnp.dot(act, wd_ref[lo:hi, :], preferred_element_type=F32)
    o_ref[...] = _layer_norm(alpha * h + f, g2_ref[...], b2_ref[...])


def _post(x2, fox_o, dsa_o, w_out, ln1_g, ln1_b, w_gate, w_up, w_down, ln2_g, ln2_b, *, alpha, tm, ff_chunk=256):
    n, d = x2.shape
    d_ff = w_gate.shape[1]
    assert n % tm == 0 and d_ff % ff_chunk == 0
    row = lambda t: (t, 0)
    const = lambda t: (0, 0)
    wspec = lambda shape: pl.BlockSpec(shape, const, pipeline_mode=pl.Buffered(1))
    vec = lambda a: a.astype(F32)[None, :]
    return pl.pallas_call(
        functools.partial(_post_kernel, alpha=alpha, ff_chunk=ff_chunk),
        grid=(n // tm,),
        in_specs=[pl.BlockSpec((tm, d), row), pl.BlockSpec((tm, FOX_WIDTH), row), pl.BlockSpec((tm, DSA_WIDTH), row),
                  wspec(w_out.shape), wspec((1, d)), wspec((1, d)),
                  wspec(w_gate.shape), wspec(w_up.shape), wspec(w_down.shape), wspec((1, d)), wspec((1, d))],
        out_specs=pl.BlockSpec((tm, d), row),
        out_shape=jax.ShapeDtypeStruct((n, d), F32),
        compiler_params=pltpu.CompilerParams(dimension_semantics=("parallel",), vmem_limit_bytes=VMEM_LIMIT),
        name="post",
    )(x2, fox_o, dsa_o, w_out.astype(BF16), vec(ln1_g), vec(ln1_b),
      w_gate.astype(BF16), w_up.astype(BF16), w_down.astype(BF16), vec(ln2_g), vec(ln2_b))


def _round_up(n, m):
    return -(-n // m) * m


def _pad_rows(a, rows):
    return jnp.pad(a, ((0, 0), (0, rows - a.shape[1]), (0, 0)))


def _largest_tile(n, candidates):
    return next(c for c in candidates if n % c == 0)


def _mixer_layer(y, past, weights, *, alpha):
    (w_packed, bias, w_out, ln1_g, ln1_b, w_gate, w_up, w_down, ln2_g, ln2_b) = weights
    b, t, d = y.shape
    p_len = 0 if past is None else past[0].shape[1]
    tables = _rope_lane_tables(p_len + jnp.arange(t))
    (fq, fk, fkb, fv, fvb, dq, dk, dv, dk2, dvb, iq, ik, ik2, misc) = _project(
        y.reshape(b * t, d), w_packed, bias, tables, t, min(512, t))
    r3 = lambda a: a.reshape(b, t, a.shape[-1])
    fq, fkb, fvb, dq, dk2, dvb, iq, ik2, misc3 = map(r3, (fq, fkb, fvb, dq, dk2, dvb, iq, ik2, misc))

    t_pad = _round_up(t, LANES)
    neg_c_new = _cumsum(_pad_rows(misc3, t_pad), scale=-LOG2E)[:, :t]
    n_keys = p_len + t
    top_k = min(IDX_TOPK_MAX, n_keys // 4)
    if past is None:
        n_pad = n_keys
        k_all, v_all, kb = fkb, fvb, neg_c_new
        dk_all, dv_all, ik_all = dk2, dvb, ik2
    else:
        cf_k, cf_v, cf_logf, cd_k, cd_v, ci_k = past
        n_pad = _round_up(n_keys, LANES)
        cat = lambda old, new: _pad_rows(jnp.concatenate([old.astype(new.dtype), new], axis=1), n_pad)
        k_all = cat(cf_k.reshape(b, p_len, FOX_WIDTH), fkb)
        v_all = cat(cf_v.reshape(b, p_len, FOX_WIDTH), fvb)
        dk_all = cat(jnp.repeat(cd_k, 2, axis=2).reshape(b, p_len, 2 * LANES), dk2)
        dv_all = cat(cd_v.reshape(b, p_len, LANES), dvb)
        ik_all = cat(jnp.concatenate([ci_k, ci_k], axis=-1), ik2)
        lf_past = jnp.pad(cf_logf.astype(F32), ((0, 0), (0, 0), (0, LANES - FOX_HEADS)))
        d_past = _cumsum(lf_past, reverse=True, inclusive=False, scale=LOG2E)
        kb = cat(d_past, neg_c_new)
    tq = min(512, t_pad)
    tk = _largest_tile(n_pad, (512, 384, 256, 128))
    pad_q = lambda a: _pad_rows(a, t_pad)
    w_t = jnp.transpose(misc3[:, :, _MISC_IW:_MISC_IW + SUBLANES], (0, 2, 1))
    w_t = jnp.pad(w_t, ((0, 0), (0, 0), (0, t_pad - t)))
    fox_o = _fox_attn(pad_q(fq), k_all, v_all, kb, q_pos0=p_len, tq=tq, tk=tk)[:, :t]
    dsa_o = _dsa_attn(pad_q(dq), pad_q(iq), w_t, dk_all, dv_all, ik_all, q_pos0=p_len, n_real=n_keys,
                      top_k=top_k, tq=tq, tk=tk)[:, :t]
    y_out = _post(y.reshape(b * t, d), fox_o.reshape(b * t, -1), dsa_o.reshape(b * t, -1),
                  w_out, ln1_g, ln1_b, w_gate, w_up, w_down, ln2_g, ln2_b, alpha=alpha, tm=min(512, b * t))
    states = (fk.reshape(b, t, FOX_HEADS, HEAD_DIM), fv.reshape(b, t, FOX_HEADS, HEAD_DIM),
              misc3[:, :, :FOX_HEADS],
              dk.reshape(b, t, DSA_KV_HEADS, HEAD_DIM), dv.reshape(b, t, DSA_KV_HEADS, HEAD_DIM),
              ik.reshape(b, t, HEAD_DIM))
    return y_out.reshape(b, t, d), states


def kernel(x_prompt, x_sample, cache_fox_k, cache_fox_v, cache_fox_logf, cache_dsa_k, cache_dsa_v, cache_idx_k,
           w_in, b_forget, w_out, ln1_g, ln1_b, w_gate, w_up, w_down, ln2_g, ln2_b):
    depth = w_in.shape[0]
    alpha = (2 * depth) ** 0.25
    yp, ys = x_prompt, x_sample
    p_states, s_states = [], []
    for l in range(depth):
        w_packed, bias = _pack_w_in(w_in[l], b_forget[l])
        weights = (w_packed, bias, w_out[l], ln1_g[l], ln1_b[l], w_gate[l], w_up[l], w_down[l], ln2_g[l], ln2_b[l])
        yp, st = _mixer_layer(yp, None, weights, alpha=alpha)
        p_states.append(st)
        past = (cache_fox_k[l], cache_fox_v[l], cache_fox_logf[l], cache_dsa_k[l], cache_dsa_v[l], cache_idx_k[l])
        ys, st = _mixer_layer(ys, past, weights, alpha=alpha)
        s_states.append(st)
    stack = lambda sts: tuple(jnp.stack([s[n] for s in sts], axis=0) for n in range(6))
    return (yp, ys) + stack(p_states) + stack(s_states)
```

```python
import functools

import numpy as np
import jax
import jax.numpy as jnp
from jax import lax
from jax.experimental import pallas as pl
from jax.experimental.pallas import tpu as pltpu

HEAD_DIM = 64
FOX_HEADS = 8
DSA_HEADS = 8
DSA_KV_HEADS = 2
IDX_HEADS = 4
CHUNK = 64
CHUNK_SHIFT = 6
IDX_TOPK_MAX = 256
ROPE_THETA = 500000.0
ROT_DIM = HEAD_DIM // 4
LN_EPS = 1e-5
LOG2E = 1.4426950408889634

LANES = 128
SUBLANES = 8
FOX_WIDTH = FOX_HEADS * HEAD_DIM
DSA_WIDTH = DSA_HEADS * HEAD_DIM
INT_MIN = -(2 ** 31)
ZERO_BAND = 1 << 13
NEG_INF = float("-inf")
F32 = jnp.float32
BF16 = jnp.bfloat16
VMEM_LIMIT = 56 * 1024 * 1024

_C_FQ, _C_FK, _C_FV, _C_DQ = 0, 512, 1024, 1536
_C_DK, _C_DV, _C_IQ, _C_IK, _C_MISC, _C_END = 2048, 2176, 2304, 2560, 2688, 2816
_MISC_IW = 8


def _nt_dot(a, b):
    return lax.dot_general(a, b, (((1,), (1,)), ((), ())), preferred_element_type=F32)


def _half_mask(shape, hh):
    lane = lax.broadcasted_iota(jnp.int32, shape, len(shape) - 1)
    return (lane < HEAD_DIM) if hh == 0 else (lane >= HEAD_DIM)


def _project_kernel(x_ref, w_ref, bias_ref, c_ref, sa_ref, sb_ref,
                    fq_ref, fk_ref, fkb_ref, fv_ref, fvb_ref, dq_ref, dk_ref, dv_ref,
                    dk2_ref, dvb_ref, iq_ref, ik_ref, ik2_ref, misc_ref):
    xb = x_ref[...].astype(BF16)
    cos, sa, sb = c_ref[...], sa_ref[...], sb_ref[...]

    def mm(lo, hi):
        return jnp.dot(xb, w_ref[:, lo:hi], preferred_element_type=F32)

    def rope(yb):
        return yb * cos + pltpu.roll(yb, LANES - ROT_DIM // 2, 1) * sa + pltpu.roll(yb, ROT_DIM // 2, 1) * sb

    def blocks(y):
        return [y[:, b * LANES:(b + 1) * LANES] for b in range(y.shape[1] // LANES)]

    fq_ref[...] = mm(_C_FQ, _C_FK).astype(BF16)
    y = mm(_C_FK, _C_FV)
    fk_ref[...] = y
    fkb_ref[...] = y.astype(BF16)
    y = mm(_C_FV, _C_DQ)
    fv_ref[...] = y
    fvb_ref[...] = y.astype(BF16)
    for b, yb in enumerate(blocks(mm(_C_DQ, _C_DK))):
        dq_ref[:, b * LANES:(b + 1) * LANES] = rope(yb).astype(BF16)

    half0 = _half_mask((x_ref.shape[0], LANES), 0)
    kv = mm(_C_DK, _C_IQ)
    k = rope(kv[:, :LANES])
    v = kv[:, LANES:]
    dk_ref[...] = k
    dv_ref[...] = v
    dvb_ref[...] = v.astype(BF16)
    sw = pltpu.roll(k, HEAD_DIM, 1)
    dk2_ref[:, :LANES] = jnp.where(half0, k, sw).astype(BF16)
    dk2_ref[:, LANES:] = jnp.where(half0, sw, k).astype(BF16)

    for b, yb in enumerate(blocks(mm(_C_IQ, _C_IK))):
        iq_ref[:, b * LANES:(b + 1) * LANES] = rope(yb).astype(BF16)
    ik = rope(mm(_C_IK, _C_MISC))
    ik_ref[...] = ik[:, :HEAD_DIM]
    ik2_ref[...] = ik.astype(BF16)

    z = mm(_C_MISC, _C_END)
    zf = z + bias_ref[...]
    logf = jnp.minimum(zf, 0.0) - jnp.log1p(jnp.exp(-jnp.abs(zf)))
    lane = lax.broadcasted_iota(jnp.int32, z.shape, 1)
    misc_ref[...] = jnp.where(lane < _MISC_IW, logf, z)


def _pack_w_in(w, b_forget):
    d = w.shape[0]
    sizes = (FOX_WIDTH, FOX_WIDTH, FOX_WIDTH, FOX_HEADS, DSA_WIDTH, DSA_KV_HEADS * HEAD_DIM,
             DSA_KV_HEADS * HEAD_DIM, IDX_HEADS * HEAD_DIM, HEAD_DIM, IDX_HEADS)
    offs = np.concatenate([[0], np.cumsum(sizes)])
    fq, fk, fv, ff, dq, dk, dv, iq, ik, iw = [w[:, offs[i]:offs[i + 1]] for i in range(len(sizes))]
    scale = HEAD_DIM ** -0.5 * LOG2E
    misc = jnp.concatenate([ff, iw, jnp.zeros((d, LANES - FOX_HEADS - IDX_HEADS), w.dtype)], axis=1)
    packed = jnp.concatenate([fq * scale, fk, fv, dq * scale, dk, dv, iq, ik, ik, misc], axis=1)
    bias = jnp.concatenate([b_forget.astype(F32), jnp.zeros((LANES - FOX_HEADS,), F32)])[None, :]
    return packed.astype(BF16), bias


def _rope_lane_tables(positions):
    half = ROT_DIM // 2
    inv_freq = ROPE_THETA ** (-jnp.arange(half, dtype=F32) * 2.0 / ROT_DIM)
    ang = positions.astype(F32)[:, None] * inv_freq[None, :]
    cos, sin = jnp.cos(ang), jnp.sin(ang)
    n = positions.shape[0]
    rest = HEAD_DIM - ROT_DIM
    c = jnp.concatenate([cos, cos, jnp.ones((n, rest), F32)], axis=1)
    sa = jnp.concatenate([-sin, jnp.zeros((n, half + rest), F32)], axis=1)
    sb = jnp.concatenate([jnp.zeros((n, half), F32), sin, jnp.zeros((n, rest), F32)], axis=1)
    rep = LANES // HEAD_DIM
    return tuple(jnp.tile(t, (1, rep)) for t in (c, sa, sb))


def _project(x2, w_packed, bias, tables, seq, tm):
    n, d = x2.shape
    assert seq % tm == 0 and n % seq == 0
    tiles_per_seq = seq // tm
    row = lambda t: (t, 0)
    pos = lambda t: (t % tiles_per_seq, 0)
    const = lambda t: (0, 0)

    def out(width, dtype):
        return jax.ShapeDtypeStruct((n, width), dtype), pl.BlockSpec((tm, width), row)

    outs = [out(FOX_WIDTH, BF16),
            out(FOX_WIDTH, F32), out(FOX_WIDTH, BF16),
            out(FOX_WIDTH, F32), out(FOX_WIDTH, BF16),
            out(DSA_WIDTH, BF16),
            out(LANES, F32), out(LANES, F32),
            out(2 * LANES, BF16), out(LANES, BF16),
            out(IDX_HEADS * HEAD_DIM, BF16),
            out(HEAD_DIM, F32), out(LANES, BF16),
            out(LANES, F32)]
    return pl.pallas_call(
        _project_kernel,
        grid=(n // tm,),
        in_specs=[pl.BlockSpec((tm, d), row),
                  pl.BlockSpec((d, _C_END), const, pipeline_mode=pl.Buffered(1)),
                  pl.BlockSpec((1, LANES), const),
                  pl.BlockSpec((tm, LANES), pos), pl.BlockSpec((tm, LANES), pos), pl.BlockSpec((tm, LANES), pos)],
        out_specs=[o[1] for o in outs],
        out_shape=[o[0] for o in outs],
        compiler_params=pltpu.CompilerParams(dimension_semantics=("parallel",), vmem_limit_bytes=VMEM_LIMIT),
        name="project",
    )(x2, w_packed, bias, *tables)


def _cumsum_kernel(x_ref, o_ref, carry_ref, *, reverse, inclusive, scale):
    @pl.when(pl.program_id(1) == 0)
    def _():
        carry_ref[...] = jnp.zeros_like(carry_ref)

    x = x_ref[0]
    tm = x.shape[0]
    hi = x.astype(BF16)
    r1 = x - hi.astype(F32)
    mid = r1.astype(BF16)
    lo = (r1 - mid.astype(F32)).astype(BF16)
    row = lax.broadcasted_iota(jnp.int32, (tm, tm), 0)
    col = lax.broadcasted_iota(jnp.int32, (tm, tm), 1)
    if reverse:
        keep = (col >= row) if inclusive else (col > row)
    else:
        keep = (col <= row) if inclusive else (col < row)
    tri = jnp.where(keep, 1.0, 0.0).astype(BF16)
    cs = (jnp.dot(tri, hi, preferred_element_type=F32) + jnp.dot(tri, mid, preferred_element_type=F32)
          + jnp.dot(tri, lo, preferred_element_type=F32))
    carry = carry_ref[0:1, :]
    o_ref[0] = scale * (cs + carry)
    carry_ref[0:1, :] = carry + jnp.sum(x, axis=0, keepdims=True)


def _cumsum(x, *, reverse=False, inclusive=True, scale=1.0, tm=512):
    b, s, w = x.shape
    tm = min(tm, s)
    assert s % tm == 0 and w == LANES
    nt = s // tm
    idx = (lambda i, t: (i, nt - 1 - t, 0)) if reverse else (lambda i, t: (i, t, 0))
    return pl.pallas_call(
        functools.partial(_cumsum_kernel, reverse=reverse, inclusive=inclusive, scale=scale),
        grid=(b, nt),
        in_specs=[pl.BlockSpec((1, tm, w), idx)],
        out_specs=pl.BlockSpec((1, tm, w), idx),
        out_shape=jax.ShapeDtypeStruct(x.shape, F32),
        scratch_shapes=[pltpu.VMEM((SUBLANES, w), F32)],
        compiler_params=pltpu.CompilerParams(dimension_semantics=("parallel", "arbitrary")),
        name="cumsum",
    )(x)


BIAS_PIECES = 3


def _bias_lane0(h):
    return HEAD_DIM if h % 2 == 0 else 0


def _mask_heads(q_ref, qm_ref, n_heads, bias_ones=False):
    for h in range(n_heads):
        qpair = q_ref[0, :, (h // 2) * LANES:(h // 2 + 1) * LANES]
        other = jnp.zeros_like(qpair)
        if bias_ones:
            lane = lax.broadcasted_iota(jnp.int32, qpair.shape, 1)
            in_bias = jnp.logical_and(lane >= _bias_lane0(h), lane < _bias_lane0(h) + BIAS_PIECES)
            other = jnp.where(in_bias, 1.0, 0.0).astype(qpair.dtype)
        qm_ref[h] = jnp.where(_half_mask(qpair.shape, h % 2), qpair, other)


ONES_ROWS = 16
VAL_ROWS = HEAD_DIM + ONES_ROWS


def _flash_scratch(n_heads, tq, tk):
    return [pltpu.VMEM((2, n_heads, tk, tq), F32), pltpu.VMEM((2, n_heads, SUBLANES, tq), F32),
            pltpu.VMEM((n_heads, 1, tq), F32), pltpu.VMEM((n_heads, VAL_ROWS, tq), F32)]


def _init_softmax(m_ref, acc_ref):
    m_ref[...] = jnp.full_like(m_ref, NEG_INF)
    acc_ref[...] = jnp.zeros_like(acc_ref)


def _flash_loop(segments, logits0, values, s_ref, st_ref, m_ref, acc_ref):
    def by_parity(j, fn):
        for slot in range(2):
            pl.when(lax.rem(j, 2) == slot)(functools.partial(fn, slot))

    def stage_a(j, logits, slot):
        for h, s in enumerate(logits(j)):
            s_ref[slot, h] = s
            m_prev = m_ref[h]
            m_new = jnp.maximum(m_prev, jnp.max(s, axis=0, keepdims=True))
            m_use = jnp.where(m_new == NEG_INF, 0.0, m_new)
            st_ref[slot, h, 0:1, :] = m_use
            st_ref[slot, h, 1:2, :] = jnp.exp2(m_prev - m_use)
            m_ref[h] = m_new

    def stage_b(j, slot):
        for h in range(m_ref.shape[0]):
            p = jnp.exp2(s_ref[slot, h] - st_ref[slot, h, 0:1, :]).astype(BF16)
            alpha = st_ref[slot, h, 1:2, :]
            acc_ref[h] = alpha * acc_ref[h] + jnp.dot(values(j, h), p, preferred_element_type=F32)

    stage_a(0, logits0, 0)
    start = 1
    for end, logits in segments:
        def both(slot, j, logits):
            stage_a(j, logits, slot)
            stage_b(j - 1, 1 - slot)

        def body(j, carry, logits=logits):
            by_parity(j, functools.partial(both, j=j, logits=logits))
            return carry

        lax.fori_loop(start, end, body, 0)
        start = jnp.maximum(start, end)
    by_parity(start - 1, lambda slot: stage_b(start - 1, slot))


def _store_heads(o_ref, acc_ref, n_heads):
    def head(h):
        return acc_ref[h, :HEAD_DIM, :] / acc_ref[h, HEAD_DIM:HEAD_DIM + 1, :]

    for c in range(n_heads // 2):
        ot = jnp.concatenate([head(2 * c), head(2 * c + 1)], axis=0)
        o_ref[0, :, c * LANES:(c + 1) * LANES] = ot.T.astype(o_ref.dtype)


def _key_major(v, tk):
    b, t, c = v.shape
    heads = c // HEAD_DIM
    vt = jnp.transpose(v.reshape(b, t // tk, tk, heads, HEAD_DIM), (0, 1, 3, 4, 2))
    ones = jnp.ones((b, t // tk, heads, ONES_ROWS, tk), v.dtype)
    return jnp.concatenate([vt, ones], axis=3).reshape(b, t // tk, heads * VAL_ROWS, tk)


def _fox_keys_kernel(k_ref, kb_ref, o_ref):
    kb = kb_ref[0]
    hi = kb.astype(BF16).astype(F32)
    r1 = kb - hi
    mid = r1.astype(BF16).astype(F32)
    pieces = (hi, mid, r1 - mid)
    lane = lax.broadcasted_iota(jnp.int32, kb.shape, 1)
    for h in range(FOX_HEADS):
        bias = jnp.zeros_like(kb)
        for n, piece in enumerate(pieces):
            dst = _bias_lane0(h) + n
            bias = jnp.where(lane == dst, pltpu.roll(piece, (dst - h) % LANES, 1), bias)
        kpair = k_ref[0, :, (h // 2) * LANES:(h // 2 + 1) * LANES]
        o_ref[0, :, h * LANES:(h + 1) * LANES] = jnp.where(_half_mask(kpair.shape, h % 2), kpair, bias.astype(BF16))


def _fox_keys(k, kb, tm=512):
    b, t, _ = k.shape
    tm = _largest_tile(t, (tm, 384, 256, 128))
    spec = lambda w: pl.BlockSpec((1, tm, w), lambda bb, i: (bb, i, 0))
    return pl.pallas_call(
        _fox_keys_kernel,
        grid=(b, t // tm),
        in_specs=[spec(FOX_WIDTH), spec(LANES)],
        out_specs=spec(FOX_HEADS * LANES),
        out_shape=jax.ShapeDtypeStruct((b, t, FOX_HEADS * LANES), BF16),
        compiler_params=pltpu.CompilerParams(dimension_semantics=("parallel", "parallel")),
        name="fox_keys",
    )(k, kb)


def _fox_kernel(q_ref, k_ref, vt_ref, o_ref, qm_ref, s_ref, st_ref, m_ref, acc_ref, *, tq, tk, q_pos0, nk):
    row0 = q_pos0 + pl.program_id(1) * tq
    n_full = jnp.minimum(nk, (row0 + 1) // tk)
    j_end = jnp.minimum(nk, (row0 + tq - 1) // tk + 1)
    _mask_heads(q_ref, qm_ref, FOX_HEADS, bias_ones=True)
    _init_softmax(m_ref, acc_ref)

    def logits(j, masked):
        off = pl.multiple_of(j * tk, tk)
        if masked:
            key_pos = j * tk + lax.broadcasted_iota(jnp.int32, (tk, tq), 0)
            visible = key_pos <= row0 + lax.broadcasted_iota(jnp.int32, (tk, tq), 1)
        for h in range(FOX_HEADS):
            s = _nt_dot(k_ref[0, pl.ds(off, tk), h * LANES:(h + 1) * LANES], qm_ref[h])
            yield jnp.where(visible, s, NEG_INF) if masked else s

    def values(j, h):
        return vt_ref[0, j, h * VAL_ROWS:(h + 1) * VAL_ROWS, :]

    masked = functools.partial(logits, masked=True)
    _flash_loop([(n_full, functools.partial(logits, masked=False)), (j_end, masked)], masked, values,
                s_ref, st_ref, m_ref, acc_ref)
    _store_heads(o_ref, acc_ref, FOX_HEADS)


def _fox_attn(q, k, v, kb, *, q_pos0, tq, tk):
    b, tq_all, _ = q.shape
    tk_all = k.shape[1]
    assert tq_all % tq == 0 and tk_all % tk == 0 and tq % LANES == 0
    nk = tk_all // tk
    vt = _key_major(v, tk)
    k_aug = _fox_keys(k, kb)
    return pl.pallas_call(
        functools.partial(_fox_kernel, tq=tq, tk=tk, q_pos0=q_pos0, nk=nk),
        grid=(b, tq_all // tq),
        in_specs=[pl.BlockSpec((1, tq, FOX_WIDTH), lambda bb, i: (bb, i, 0)),
                  pl.BlockSpec((1, tk_all, FOX_HEADS * LANES), lambda bb, i: (bb, 0, 0),
                               pipeline_mode=pl.Buffered(1)),
                  pl.BlockSpec((1, nk, FOX_HEADS * VAL_ROWS, tk), lambda bb, i: (bb, 0, 0, 0),
                               pipeline_mode=pl.Buffered(1))],
        out_specs=pl.BlockSpec((1, tq, FOX_WIDTH), lambda bb, i: (bb, i, 0)),
        out_shape=jax.ShapeDtypeStruct(q.shape, BF16),
        scratch_shapes=[pltpu.VMEM((FOX_HEADS, tq, LANES), BF16)] + _flash_scratch(FOX_HEADS, tq, tk),
        compiler_params=pltpu.CompilerParams(dimension_semantics=("parallel", "arbitrary"),
                                             vmem_limit_bytes=VMEM_LIMIT),
        name="fox_attn",
    )(q, k_aug, vt)


def _dsa_kernel(dq_ref, iq_ref, w_ref, dk_ref, dvt_ref, ik_ref, o_ref,
                qm_ref, iqm_ref, key_ref, hi_ref, lo_ref, drop_ref, s_ref, st_ref, m_ref, acc_ref,
                *, tq, tk, q_pos0, n_real, top_k, nk):
    row0 = q_pos0 + pl.program_id(1) * tq
    q_pos = row0 + lax.broadcasted_iota(jnp.int32, (1, tq), 1)
    lim = jnp.minimum(lax.shift_left(jnp.right_shift(q_pos, CHUNK_SHIFT) + 1, CHUNK_SHIFT), n_real)
    lim_first = jnp.minimum((row0 // CHUNK + 1) * CHUNK, n_real)
    lim_last = jnp.minimum(((row0 + tq - 1) // CHUNK + 1) * CHUNK, n_real)
    n_full = lim_first // tk
    nt = (lim_last + tk - 1) // tk
    _mask_heads(dq_ref, qm_ref, DSA_HEADS)
    _mask_heads(iq_ref, iqm_ref, IDX_HEADS)

    def ktile(ref, j, lo, hi):
        return ref[0, pl.ds(pl.multiple_of(j * tk, tk), tk), lo:hi]

    def key_pos(j):
        return j * tk + lax.broadcasted_iota(jnp.int32, (tk, tq), 0)

    def score_tile(j, edge):
        ikt = ktile(ik_ref, j, 0, LANES)
        sc = jnp.zeros((tk, tq), F32)
        for h in range(IDX_HEADS):
            sc = sc + w_ref[0, h:h + 1, :] * jnp.maximum(_nt_dot(ikt, iqm_ref[h]), 0.0)
        bits = pltpu.bitcast(sc, jnp.int32)
        key = jnp.where(bits < 0, (bits ^ 0x7FFFFFFF) + 1 - ZERO_BAND, bits)
        pos = key_pos(j)
        key = jnp.where(sc == 0.0, -1 - pos, key)
        if edge:
            key = jnp.where(pos < lim, key, INT_MIN)
        key_ref[j] = key
        hi_ref[j] = jnp.right_shift(key, 16).astype(jnp.int16)
        lo_ref[j] = (jnp.bitwise_and(key, 0xFFFF) - 32768).astype(jnp.int16)

    def score_full(j, carry):
        score_tile(j, False)
        return carry

    def score_edge(j, carry):
        score_tile(j, True)
        return carry

    lax.fori_loop(0, n_full, score_full, 0)
    lax.fori_loop(n_full, nt, score_edge, 0)

    def count16(ref, cand):
        cand16 = cand.astype(jnp.int16)

        def body(j, acc):
            hit = jnp.where(ref[j] >= cand16, jnp.asarray(1, BF16), jnp.asarray(0, BF16))
            parts = [hit[g * 16:(g + 1) * 16, :] for g in range(tk // 16)]
            while len(parts) > 1:
                pairs = [a + b for a, b in zip(parts[0::2], parts[1::2])]
                parts = pairs + parts[len(pairs) * 2:]
            return acc + parts[0].astype(F32)

        acc = lax.fori_loop(0, nt, body, jnp.zeros((16, tq), F32))
        return jnp.sum(acc, axis=0, keepdims=True)

    def bisect16(ref, base, cnt0):
        def step(it, carry):
            t, cnt_t = carry
            cand = t + lax.shift_left(jnp.int32(1), 15 - it)
            cnt = base + count16(ref, cand)
            ok = cnt >= top_k
            return jnp.where(ok, cand, t), jnp.where(ok, cnt, cnt_t)
        return lax.fori_loop(0, 16, step, (jnp.full((1, tq), -32768, jnp.int32), cnt0))

    t_hi, cnt_hi = bisect16(hi_ref, 0.0, jnp.full((1, tq), 2.0 ** 30, F32))
    above = jnp.where(t_hi >= 32767, 0.0, count16(hi_ref, jnp.minimum(t_hi + 1, 32767)))
    t_hi16 = t_hi.astype(jnp.int16)

    def keep_low(j, carry):
        lo_ref[j] = jnp.where(hi_ref[j] == t_hi16, lo_ref[j], jnp.asarray(-32768, jnp.int16))
        return carry

    lax.fori_loop(0, nt, keep_low, 0)
    t_lo, cnt_thr = bisect16(lo_ref, above, cnt_hi)
    thr = t_hi * 65536 + (t_lo + 32768)

    def count(pred):
        def body(j, acc):
            hit = jnp.where(pred(key_ref[j], j), 1.0, 0.0)
            return acc + jnp.sum(hit.reshape(tk // SUBLANES, SUBLANES, tq), axis=0)
        acc = lax.fori_loop(0, nt, body, jnp.zeros((SUBLANES, tq), F32))
        return jnp.sum(acc, axis=0, keepdims=True)

    excess = jnp.logical_and(cnt_thr > top_k, thr > INT_MIN)

    @pl.when(jnp.max(jnp.where(excess, 1.0, 0.0)) > 0.0)
    def _():
        need = top_k - count(lambda kt, j: kt > thr)
        bound = jnp.zeros((1, tq), jnp.int32)
        for bit in reversed(range(int(nk * tk - 1).bit_length())):
            cand = bound + (1 << bit)
            c = count(lambda kt, j: jnp.logical_and(kt == thr, key_pos(j) < cand))
            bound = jnp.where(c < need, cand, bound)

        def demote(j, carry):
            kt = key_ref[j]
            drop = jnp.logical_and(jnp.logical_and(kt == thr, key_pos(j) > bound), excess)
            key_ref[j] = jnp.where(drop, kt - 1, kt)
            return carry

        lax.fori_loop(0, nt, demote, 0)

    thr_sel = jnp.maximum(thr, INT_MIN + 1)
    _init_softmax(m_ref, acc_ref)
    kv_head = lambda h: h // (DSA_HEADS // DSA_KV_HEADS)

    def logits(j):
        drop_ref[...] = jnp.where(key_ref[j] >= thr_sel, 0.0, NEG_INF)
        for h in range(DSA_HEADS):
            g = kv_head(h)
            yield _nt_dot(ktile(dk_ref, j, g * LANES, (g + 1) * LANES), qm_ref[h]) + drop_ref[...]

    def values(j, h):
        return dvt_ref[0, j, kv_head(h) * VAL_ROWS:(kv_head(h) + 1) * VAL_ROWS, :]

    _flash_loop([(nt, logits)], logits, values, s_ref, st_ref, m_ref, acc_ref)
    _store_heads(o_ref, acc_ref, DSA_HEADS)


def _dsa_attn(dq, iq, w_t, dk2, dv, ik2, *, q_pos0, n_real, top_k, tq, tk):
    b, tq_all, _ = dq.shape
    tk_all = dk2.shape[1]
    assert tq_all % tq == 0 and tk_all % tk == 0 and tq % LANES == 0 and tk % 16 == 0 and tk_all <= ZERO_BAND
    nk = tk_all // tk
    dvt = _key_major(dv, tk)
    qspec = lambda w: pl.BlockSpec((1, tq, w), lambda bb, i: (bb, i, 0))
    kspec = lambda w: pl.BlockSpec((1, tk_all, w), lambda bb, i: (bb, 0, 0))
    return pl.pallas_call(
        functools.partial(_dsa_kernel, tq=tq, tk=tk, q_pos0=q_pos0, n_real=n_real, top_k=top_k, nk=nk),
        grid=(b, tq_all // tq),
        in_specs=[qspec(DSA_WIDTH), qspec(IDX_HEADS * HEAD_DIM),
                  pl.BlockSpec((1, SUBLANES, tq), lambda bb, i: (bb, 0, i)),
                  kspec(2 * LANES),
                  pl.BlockSpec((1, nk, DSA_KV_HEADS * VAL_ROWS, tk), lambda bb, i: (bb, 0, 0, 0)),
                  kspec(LANES)],
        out_specs=qspec(DSA_WIDTH),
        out_shape=jax.ShapeDtypeStruct(dq.shape, BF16),
        scratch_shapes=[pltpu.VMEM((DSA_HEADS, tq, LANES), BF16), pltpu.VMEM((IDX_HEADS, tq, LANES), BF16),
                        pltpu.VMEM((nk, tk, tq), jnp.int32), pltpu.VMEM((nk, tk, tq), jnp.int16),
                        pltpu.VMEM((nk, tk, tq), jnp.int16), pltpu.VMEM((tk, tq), F32)]
                       + _flash_scratch(DSA_HEADS, tq, tk),
        compiler_params=pltpu.CompilerParams(dimension_semantics=("parallel", "arbitrary"),
                                             vmem_limit_bytes=VMEM_LIMIT),
        name="dsa_attn",
    )(dq, iq, w_t, dk2, dvt, ik2)


def _layer_norm(x, g, b):
    mu = jnp.mean(x, axis=-1, keepdims=True)
    xc = x - mu
    var = jnp.mean(xc * xc, axis=-1, keepdims=True)
    return xc * lax.rsqrt(var + LN_EPS) * g + b


def _post_kernel(x_ref, fox_ref, dsa_ref, wo_ref, g1_ref, b1_ref, wg_ref, wu_ref, wd_ref, g2_ref, b2_ref,
                 o_ref, *, alpha, ff_chunk):
    mix = (jnp.dot(fox_ref[...], wo_ref[:FOX_WIDTH, :], preferred_element_type=F32)
           + jnp.dot(dsa_ref[...], wo_ref[FOX_WIDTH:, :], preferred_element_type=F32))
    h = _layer_norm(alpha * x_ref[...] + mix, g1_ref[...], b1_ref[...])
    hb = h.astype(BF16)
    f = jnp.zeros_like(h)
    for c in range(wg_ref.shape[1] // ff_chunk):
        lo, hi = c * ff_chunk, (c + 1) * ff_chunk
        gate = jnp.dot(hb, wg_ref[:, lo:hi], preferred_element_type=F32)
        up = jnp.dot(hb, wu_ref[:, lo:hi], preferred_element_type=F32)
        act = (gate * jax.nn.sigmoid(gate) * up).astype(BF16)
        f = f + jnp.dot(act, wd_ref[lo:hi, :], preferred_element_type=F32)
    o_ref[...] = _layer_norm(alpha * h + f, g2_ref[...], b2_ref[...])


def _post(x2, fox_o, dsa_o, w_out, ln1_g, ln1_b, w_gate, w_up, w_down, ln2_g, ln2_b, *, alpha, tm, ff_chunk=256):
    n, d = x2.shape
    d_ff = w_gate.shape[1]
    assert n % tm == 0 and d_ff % ff_chunk == 0
    row = lambda t: (t, 0)
    const = lambda t: (0, 0)
    wspec = lambda shape: pl.BlockSpec(shape, const, pipeline_mode=pl.Buffered(1))
    vec = lambda a: a.astype(F32)[None, :]
    return pl.pallas_call(
        functools.partial(_post_kernel, alpha=alpha, ff_chunk=ff_chunk),
        grid=(n // tm,),
        in_specs=[pl.BlockSpec((tm, d), row), pl.BlockSpec((tm, FOX_WIDTH), row), pl.BlockSpec((tm, DSA_WIDTH), row),
                  wspec(w_out.shape), wspec((1, d)), wspec((1, d)),
                  wspec(w_gate.shape), wspec(w_up.shape), wspec(w_down.shape), wspec((1, d)), wspec((1, d))],
        out_specs=pl.BlockSpec((tm, d), row),
        out_shape=jax.ShapeDtypeStruct((n, d), F32),
        compiler_params=pltpu.CompilerParams(dimension_semantics=("parallel",), vmem_limit_bytes=VMEM_LIMIT),
        name="post",
    )(x2, fox_o, dsa_o, w_out.astype(BF16), vec(ln1_g), vec(ln1_b),
      w_gate.astype(BF16), w_up.astype(BF16), w_down.astype(BF16), vec(ln2_g), vec(ln2_b))


def _round_up(n, m):
    return -(-n // m) * m


def _pad_rows(a, rows):
    return jnp.pad(a, ((0, 0), (0, rows - a.shape[1]), (0, 0)))


def _largest_tile(n, candidates):
    return next(c for c in candidates if n % c == 0)


def _mixer_layer(y, past, weights, *, alpha):
    (w_packed, bias, w_out, ln1_g, ln1_b, w_gate, w_up, w_down, ln2_g, ln2_b) = weights
    b, t, d = y.shape
    p_len = 0 if past is None else past[0].shape[1]
    tables = _rope_lane_tables(p_len + jnp.arange(t))
    (fq, fk, fkb, fv, fvb, dq, dk, dv, dk2, dvb, iq, ik, ik2, misc) = _project(
        y.reshape(b * t, d), w_packed, bias, tables, t, min(512, t))
    r3 = lambda a: a.reshape(b, t, a.shape[-1])
    fq, fkb, fvb, dq, dk2, dvb, iq, ik2, misc3 = map(r3, (fq, fkb, fvb, dq, dk2, dvb, iq, ik2, misc))

    t_pad = _round_up(t, LANES)
    neg_c_new = _cumsum(_pad_rows(misc3, t_pad), scale=-LOG2E)[:, :t]
    n_keys = p_len + t
    top_k = min(IDX_TOPK_MAX, n_keys // 4)
    if past is None:
        n_pad = n_keys
        k_all, v_all, kb = fkb, fvb, neg_c_new
        dk_all, dv_all, ik_all = dk2, dvb, ik2
    else:
        cf_k, cf_v, cf_logf, cd_k, cd_v, ci_k = past
        n_pad = _round_up(n_keys, LANES)
        cat = lambda old, new: _pad_rows(jnp.concatenate([old.astype(new.dtype), new], axis=1), n_pad)
        k_all = cat(cf_k.reshape(b, p_len, FOX_WIDTH), fkb)
        v_all = cat(cf_v.reshape(b, p_len, FOX_WIDTH), fvb)
        dk_all = cat(jnp.repeat(cd_k, 2, axis=2).reshape(b, p_len, 2 * LANES), dk2)
        dv_all = cat(cd_v.reshape(b, p_len, LANES), dvb)
        ik_all = cat(jnp.concatenate([ci_k, ci_k], axis=-1), ik2)
        lf_past = jnp.pad(cf_logf.astype(F32), ((0, 0), (0, 0), (0, LANES - FOX_HEADS)))
        d_past = _cumsum(lf_past, reverse=True, inclusive=False, scale=LOG2E)
        kb = cat(d_past, neg_c_new)
    tq = min(512, t_pad)
    tk = _largest_tile(n_pad, (512, 384, 256, 128))
    pad_q = lambda a: _pad_rows(a, t_pad)
    w_t = jnp.transpose(misc3[:, :, _MISC_IW:_MISC_IW + SUBLANES], (0, 2, 1))
    w_t = jnp.pad(w_t, ((0, 0), (0, 0), (0, t_pad - t)))
    fox_o = _fox_attn(pad_q(fq), k_all, v_all, kb, q_pos0=p_len, tq=tq, tk=tk)[:, :t]
    dsa_o = _dsa_attn(pad_q(dq), pad_q(iq), w_t, dk_all, dv_all, ik_all, q_pos0=p_len, n_real=n_keys,
                      top_k=top_k, tq=tq, tk=tk)[:, :t]
    y_out = _post(y.reshape(b * t, d), fox_o.reshape(b * t, -1), dsa_o.reshape(b * t, -1),
                  w_out, ln1_g, ln1_b, w_gate, w_up, w_down, ln2_g, ln2_b, alpha=alpha, tm=min(512, b * t))
    states = (fk.reshape(b, t, FOX_HEADS, HEAD_DIM), fv.reshape(b, t, FOX_HEADS, HEAD_DIM),
              misc3[:, :, :FOX_HEADS],
              dk.reshape(b, t, DSA_KV_HEADS, HEAD_DIM), dv.reshape(b, t, DSA_KV_HEADS, HEAD_DIM),
              ik.reshape(b, t, HEAD_DIM))
    return y_out.reshape(b, t, d), states


def kernel(x_prompt, x_sample, cache_fox_k, cache_fox_v, cache_fox_logf, cache_dsa_k, cache_dsa_v, cache_idx_k,
           w_in, b_forget, w_out, ln1_g, ln1_b, w_gate, w_up, w_down, ln2_g, ln2_b):
    depth = w_in.shape[0]
    alpha = (2 * depth) ** 0.25
    yp, ys = x_prompt, x_sample
    p_states, s_states = [], []
    for l in range(depth):
        w_packed, bias = _pack_w_in(w_in[l], b_forget[l])
        weights = (w_packed, bias, w_out[l], ln1_g[l], ln1_b[l], w_gate[l], w_up[l], w_down[l], ln2_g[l], ln2_b[l])
        yp, st = _mixer_layer(yp, None, weights, alpha=alpha)
        p_states.append(st)
        past = (cache_fox_k[l], cache_fox_v[l], cache_fox_logf[l], cache_dsa_k[l], cache_dsa_v[l], cache_idx_k[l])
        ys, st = _mixer_layer(ys, past, weights, alpha=alpha)
        s_states.append(st)
    stack = lambda sts: tuple(jnp.stack([s[n] for s in sts], axis=0) for n in range(6))
    return (yp, ys) + stack(p_states) + stack(s_states)
```

```python
import functools

import numpy as np
import jax
import jax.numpy as jnp
from jax import lax
from jax.experimental import pallas as pl
from jax.experimental.pallas import tpu as pltpu

HEAD_DIM = 64
FOX_HEADS = 8
DSA_HEADS = 8
DSA_KV_HEADS = 2
IDX_HEADS = 4
CHUNK = 64
CHUNK_SHIFT = 6
IDX_TOPK_MAX = 256
ROPE_THETA = 500000.0
ROT_DIM = HEAD_DIM // 4
LN_EPS = 1e-5
LOG2E = 1.4426950408889634

LANES = 128
SUBLANES = 8
FOX_WIDTH = FOX_HEADS * HEAD_DIM
DSA_WIDTH = DSA_HEADS * HEAD_DIM
INT_MIN = -(2 ** 31)
ZERO_BAND = 1 << 13
NEG_INF = float("-inf")
F32 = jnp.float32
BF16 = jnp.bfloat16
VMEM_LIMIT = 56 * 1024 * 1024

_C_FQ, _C_FK, _C_FV, _C_DQ = 0, 512, 1024, 1536
_C_DK, _C_DV, _C_IQ, _C_IK, _C_MISC, _C_END = 2048, 2176, 2304, 2560, 2688, 2816
_MISC_IW = 8


def _nt_dot(a, b):
    return lax.dot_general(a, b, (((1,), (1,)), ((), ())), preferred_element_type=F32)


def _half_mask(shape, hh):
    lane = lax.broadcasted_iota(jnp.int32, shape, len(shape) - 1)
    return (lane < HEAD_DIM) if hh == 0 else (lane >= HEAD_DIM)


def _project_kernel(x_ref, w_ref, bias_ref, c_ref, sa_ref, sb_ref,
                    fq_ref, fk_ref, fkb_ref, fv_ref, fvb_ref, dq_ref, dk_ref, dv_ref,
                    dk2_ref, dvb_ref, iq_ref, ik_ref, ik2_ref, misc_ref):
    xb = x_ref[...].astype(BF16)
    cos, sa, sb = c_ref[...], sa_ref[...], sb_ref[...]

    def mm(lo, hi):
        return jnp.dot(xb, w_ref[:, lo:hi], preferred_element_type=F32)

    def rope(yb):
        return yb * cos + pltpu.roll(yb, LANES - ROT_DIM // 2, 1) * sa + pltpu.roll(yb, ROT_DIM // 2, 1) * sb

    def blocks(y):
        return [y[:, b * LANES:(b + 1) * LANES] for b in range(y.shape[1] // LANES)]

    fq_ref[...] = mm(_C_FQ, _C_FK).astype(BF16)
    y = mm(_C_FK, _C_FV)
    fk_ref[...] = y
    fkb_ref[...] = y.astype(BF16)
    y = mm(_C_FV, _C_DQ)
    fv_ref[...] = y
    fvb_ref[...] = y.astype(BF16)
    for b, yb in enumerate(blocks(mm(_C_DQ, _C_DK))):
        dq_ref[:, b * LANES:(b + 1) * LANES] = rope(yb).astype(BF16)

    half0 = _half_mask((x_ref.shape[0], LANES), 0)
    kv = mm(_C_DK, _C_IQ)
    k = rope(kv[:, :LANES])
    v = kv[:, LANES:]
    dk_ref[...] = k
    dv_ref[...] = v
    dvb_ref[...] = v.astype(BF16)
    sw = pltpu.roll(k, HEAD_DIM, 1)
    dk2_ref[:, :LANES] = jnp.where(half0, k, sw).astype(BF16)
    dk2_ref[:, LANES:] = jnp.where(half0, sw, k).astype(BF16)

    for b, yb in enumerate(blocks(mm(_C_IQ, _C_IK))):
        iq_ref[:, b * LANES:(b + 1) * LANES] = rope(yb).astype(BF16)
    ik = rope(mm(_C_IK, _C_MISC))
    ik_ref[...] = ik[:, :HEAD_DIM]
    ik2_ref[...] = ik.astype(BF16)

    z = mm(_C_MISC, _C_END)
    zf = z + bias_ref[...]
    logf = jnp.minimum(zf, 0.0) - jnp.log1p(jnp.exp(-jnp.abs(zf)))
    lane = lax.broadcasted_iota(jnp.int32, z.shape, 1)
    misc_ref[...] = jnp.where(lane < _MISC_IW, logf, z)


def _pack_w_in(w, b_forget):
    d = w.shape[0]
    sizes = (FOX_WIDTH, FOX_WIDTH, FOX_WIDTH, FOX_HEADS, DSA_WIDTH, DSA_KV_HEADS * HEAD_DIM,
             DSA_KV_HEADS * HEAD_DIM, IDX_HEADS * HEAD_DIM, HEAD_DIM, IDX_HEADS)
    offs = np.concatenate([[0], np.cumsum(sizes)])
    fq, fk, fv, ff, dq, dk, dv, iq, ik, iw = [w[:, offs[i]:offs[i + 1]] for i in range(len(sizes))]
    scale = HEAD_DIM ** -0.5 * LOG2E
    misc = jnp.concatenate([ff, iw, jnp.zeros((d, LANES - FOX_HEADS - IDX_HEADS), w.dtype)], axis=1)
    packed = jnp.concatenate([fq * scale, fk, fv, dq * scale, dk, dv, iq, ik, ik, misc], axis=1)
    bias = jnp.concatenate([b_forget.astype(F32), jnp.zeros((LANES - FOX_HEADS,), F32)])[None, :]
    return packed.astype(BF16), bias


def _rope_lane_tables(positions):
    half = ROT_DIM // 2
    inv_freq = ROPE_THETA ** (-jnp.arange(half, dtype=F32) * 2.0 / ROT_DIM)
    ang = positions.astype(F32)[:, None] * inv_freq[None, :]
    cos, sin = jnp.cos(ang), jnp.sin(ang)
    n = positions.shape[0]
    rest = HEAD_DIM - ROT_DIM
    c = jnp.concatenate([cos, cos, jnp.ones((n, rest), F32)], axis=1)
    sa = jnp.concatenate([-sin, jnp.zeros((n, half + rest), F32)], axis=1)
    sb = jnp.concatenate([jnp.zeros((n, half), F32), sin, jnp.zeros((n, rest), F32)], axis=1)
    rep = LANES // HEAD_DIM
    return tuple(jnp.tile(t, (1, rep)) for t in (c, sa, sb))


def _project(x2, w_packed, bias, tables, seq, tm):
    n, d = x2.shape
    assert seq % tm == 0 and n % seq == 0
    tiles_per_seq = seq // tm
    row = lambda t: (t, 0)
    pos = lambda t: (t % tiles_per_seq, 0)
    const = lambda t: (0, 0)

    def out(width, dtype):
        return jax.ShapeDtypeStruct((n, width), dtype), pl.BlockSpec((tm, width), row)

    outs = [out(FOX_WIDTH, BF16),
            out(FOX_WIDTH, F32), out(FOX_WIDTH, BF16),
            out(FOX_WIDTH, F32), out(FOX_WIDTH, BF16),
            out(DSA_WIDTH, BF16),
            out(LANES, F32), out(LANES, F32),
            out(2 * LANES, BF16), out(LANES, BF16),
            out(IDX_HEADS * HEAD_DIM, BF16),
            out(HEAD_DIM, F32), out(LANES, BF16),
            out(LANES, F32)]
    return pl.pallas_call(
        _project_kernel,
        grid=(n // tm,),
        in_specs=[pl.BlockSpec((tm, d), row),
                  pl.BlockSpec((d, _C_END), const, pipeline_mode=pl.Buffered(1)),
                  pl.BlockSpec((1, LANES), const),
                  pl.BlockSpec((tm, LANES), pos), pl.BlockSpec((tm, LANES), pos), pl.BlockSpec((tm, LANES), pos)],
        out_specs=[o[1] for o in outs],
        out_shape=[o[0] for o in outs],
        compiler_params=pltpu.CompilerParams(dimension_semantics=("parallel",), vmem_limit_bytes=VMEM_LIMIT),
        name="project",
    )(x2, w_packed, bias, *tables)


def _cumsum_kernel(x_ref, o_ref, carry_ref, *, reverse, inclusive, scale):
    @pl.when(pl.program_id(1) == 0)
    def _():
        carry_ref[...] = jnp.zeros_like(carry_ref)

    x = x_ref[0]
    tm = x.shape[0]
    hi = x.astype(BF16)
    r1 = x - hi.astype(F32)
    mid = r1.astype(BF16)
    lo = (r1 - mid.astype(F32)).astype(BF16)
    row = lax.broadcasted_iota(jnp.int32, (tm, tm), 0)
    col = lax.broadcasted_iota(jnp.int32, (tm, tm), 1)
    if reverse:
        keep = (col >= row) if inclusive else (col > row)
    else:
        keep = (col <= row) if inclusive else (col < row)
    tri = jnp.where(keep, 1.0, 0.0).astype(BF16)
    cs = (jnp.dot(tri, hi, preferred_element_type=F32) + jnp.dot(tri, mid, preferred_element_type=F32)
          + jnp.dot(tri, lo, preferred_element_type=F32))
    carry = carry_ref[0:1, :]
    o_ref[0] = scale * (cs + carry)
    carry_ref[0:1, :] = carry + jnp.sum(x, axis=0, keepdims=True)


def _cumsum(x, *, reverse=False, inclusive=True, scale=1.0, tm=512):
    b, s, w = x.shape
    tm = min(tm, s)
    assert s % tm == 0 and w == LANES
    nt = s // tm
    idx = (lambda i, t: (i, nt - 1 - t, 0)) if reverse else (lambda i, t: (i, t, 0))
    return pl.pallas_call(
        functools.partial(_cumsum_kernel, reverse=reverse, inclusive=inclusive, scale=scale),
        grid=(b, nt),
        in_specs=[pl.BlockSpec((1, tm, w), idx)],
        out_specs=pl.BlockSpec((1, tm, w), idx),
        out_shape=jax.ShapeDtypeStruct(x.shape, F32),
        scratch_shapes=[pltpu.VMEM((SUBLANES, w), F32)],
        compiler_params=pltpu.CompilerParams(dimension_semantics=("parallel", "arbitrary")),
        name="cumsum",
    )(x)


BIAS_PIECES = 3


def _bias_lane0(h):
    return HEAD_DIM if h % 2 == 0 else 0


def _mask_heads(q_ref, qm_ref, n_heads, bias_ones=False):
    for h in range(n_heads):
        qpair = q_ref[0, :, (h // 2) * LANES:(h // 2 + 1) * LANES]
        other = jnp.zeros_like(qpair)
        if bias_ones:
            lane = lax.broadcasted_iota(jnp.int32, qpair.shape, 1)
            in_bias = jnp.logical_and(lane >= _bias_lane0(h), lane < _bias_lane0(h) + BIAS_PIECES)
            other = jnp.where(in_bias, 1.0, 0.0).astype(qpair.dtype)
        qm_ref[h] = jnp.where(_half_mask(qpair.shape, h % 2), qpair, other)


ONES_ROWS = 16
VAL_ROWS = HEAD_DIM + ONES_ROWS


def _flash_scratch(n_heads, tq, tk):
    return [pltpu.VMEM((2, n_heads, tk, tq), F32), pltpu.VMEM((2, n_heads, SUBLANES, tq), F32),
            pltpu.VMEM((n_heads, 1, tq), F32), pltpu.VMEM((n_heads, VAL_ROWS, tq), F32)]


def _init_softmax(m_ref, acc_ref):
    m_ref[...] = jnp.full_like(m_ref, NEG_INF)
    acc_ref[...] = jnp.zeros_like(acc_ref)


def _flash_loop(segments, logits0, values, s_ref, st_ref, m_ref, acc_ref):
    def by_parity(j, fn):
        for slot in range(2):
            pl.when(lax.rem(j, 2) == slot)(functools.partial(fn, slot))

    n_heads = m_ref.shape[0]

    def stage_a(h, s, slot):
        s_ref[slot, h] = s
        m_prev = m_ref[h]
        m_new = jnp.maximum(m_prev, jnp.max(s, axis=0, keepdims=True))
        m_use = jnp.where(m_new == NEG_INF, 0.0, m_new)
        st_ref[slot, h, 0:1, :] = m_use
        st_ref[slot, h, 1:2, :] = jnp.exp2(m_prev - m_use)
        m_ref[h] = m_new

    def stage_b(j, h, slot):
        p = jnp.exp2(s_ref[slot, h] - st_ref[slot, h, 0:1, :]).astype(BF16)
        alpha = st_ref[slot, h, 1:2, :]
        acc_ref[h] = alpha * acc_ref[h] + jnp.dot(values(j, h), p, preferred_element_type=F32)

    for h, s in enumerate(logits0(0)):
        stage_a(h, s, 0)
    start = 1
    for end, logits in segments:
        def both(slot, j, logits):
            tiles = logits(j)
            for h in range(n_heads):
                stage_b(j - 1, h, 1 - slot)
                stage_a(h, next(tiles), slot)

        def body(j, carry, logits=logits):
            by_parity(j, functools.partial(both, j=j, logits=logits))
            return carry

        lax.fori_loop(start, end, body, 0)
        start = jnp.maximum(start, end)
    def drain(slot):
        for h in range(n_heads):
            stage_b(start - 1, h, slot)

    by_parity(start - 1, drain)


def _store_heads(o_ref, acc_ref, n_heads):
    def head(h):
        return acc_ref[h, :HEAD_DIM, :] / acc_ref[h, HEAD_DIM:HEAD_DIM + 1, :]

    for c in range(n_heads // 2):
        ot = jnp.concatenate([head(2 * c), head(2 * c + 1)], axis=0)
        o_ref[0, :, c * LANES:(c + 1) * LANES] = ot.T.astype(o_ref.dtype)


def _key_major(v, tk):
    b, t, c = v.shape
    heads = c // HEAD_DIM
    vt = jnp.transpose(v.reshape(b, t // tk, tk, heads, HEAD_DIM), (0, 1, 3, 4, 2))
    ones = jnp.ones((b, t // tk, heads, ONES_ROWS, tk), v.dtype)
    return jnp.concatenate([vt, ones], axis=3).reshape(b, t // tk, heads * VAL_ROWS, tk)


def _fox_keys_kernel(k_ref, kb_ref, o_ref):
    kb = kb_ref[0]
    hi = kb.astype(BF16).astype(F32)
    r1 = kb - hi
    mid = r1.astype(BF16).astype(F32)
    pieces = (hi, mid, r1 - mid)
    lane = lax.broadcasted_iota(jnp.int32, kb.shape, 1)
    for h in range(FOX_HEADS):
        bias = jnp.zeros_like(kb)
        for n, piece in enumerate(pieces):
            dst = _bias_lane0(h) + n
            bias = jnp.where(lane == dst, pltpu.roll(piece, (dst - h) % LANES, 1), bias)
        kpair = k_ref[0, :, (h // 2) * LANES:(h // 2 + 1) * LANES]
        o_ref[0, :, h * LANES:(h + 1) * LANES] = jnp.where(_half_mask(kpair.shape, h % 2), kpair, bias.astype(BF16))


def _fox_keys(k, kb, tm=512):
    b, t, _ = k.shape
    tm = _largest_tile(t, (tm, 384, 256, 128))
    spec = lambda w: pl.BlockSpec((1, tm, w), lambda bb, i: (bb, i, 0))
    return pl.pallas_call(
        _fox_keys_kernel,
        grid=(b, t // tm),
        in_specs=[spec(FOX_WIDTH), spec(LANES)],
        out_specs=spec(FOX_HEADS * LANES),
        out_shape=jax.ShapeDtypeStruct((b, t, FOX_HEADS * LANES), BF16),
        compiler_params=pltpu.CompilerParams(dimension_semantics=("parallel", "parallel")),
        name="fox_keys",
    )(k, kb)


def _fox_kernel(q_ref, k_ref, vt_ref, o_ref, qm_ref, s_ref, st_ref, m_ref, acc_ref, *, tq, tk, q_pos0, nk):
    row0 = q_pos0 + pl.program_id(1) * tq
    n_full = jnp.minimum(nk, (row0 + 1) // tk)
    j_end = jnp.minimum(nk, (row0 + tq - 1) // tk + 1)
    _mask_heads(q_ref, qm_ref, FOX_HEADS, bias_ones=True)
    _init_softmax(m_ref, acc_ref)

    def logits(j, masked):
        off = pl.multiple_of(j * tk, tk)
        if masked:
            key_pos = j * tk + lax.broadcasted_iota(jnp.int32, (tk, tq), 0)
            visible = key_pos <= row0 + lax.broadcasted_iota(jnp.int32, (tk, tq), 1)
        for h in range(FOX_HEADS):
            s = _nt_dot(k_ref[0, pl.ds(off, tk), h * LANES:(h + 1) * LANES], qm_ref[h])
            yield jnp.where(visible, s, NEG_INF) if masked else s

    def values(j, h):
        return vt_ref[0, j, h * VAL_ROWS:(h + 1) * VAL_ROWS, :]

    masked = functools.partial(logits, masked=True)
    _flash_loop([(n_full, functools.partial(logits, masked=False)), (j_end, masked)], masked, values,
                s_ref, st_ref, m_ref, acc_ref)
    _store_heads(o_ref, acc_ref, FOX_HEADS)


def _fox_attn(q, k, v, kb, *, q_pos0, tq, tk):
    b, tq_all, _ = q.shape
    tk_all = k.shape[1]
    assert tq_all % tq == 0 and tk_all % tk == 0 and tq % LANES == 0
    nk = tk_all // tk
    vt = _key_major(v, tk)
    k_aug = _fox_keys(k, kb)
    return pl.pallas_call(
        functools.partial(_fox_kernel, tq=tq, tk=tk, q_pos0=q_pos0, nk=nk),
        grid=(b, tq_all // tq),
        in_specs=[pl.BlockSpec((1, tq, FOX_WIDTH), lambda bb, i: (bb, i, 0)),
                  pl.BlockSpec((1, tk_all, FOX_HEADS * LANES), lambda bb, i: (bb, 0, 0),
                               pipeline_mode=pl.Buffered(1)),
                  pl.BlockSpec((1, nk, FOX_HEADS * VAL_ROWS, tk), lambda bb, i: (bb, 0, 0, 0),
                               pipeline_mode=pl.Buffered(1))],
        out_specs=pl.BlockSpec((1, tq, FOX_WIDTH), lambda bb, i: (bb, i, 0)),
        out_shape=jax.ShapeDtypeStruct(q.shape, BF16),
        scratch_shapes=[pltpu.VMEM((FOX_HEADS, tq, LANES), BF16)] + _flash_scratch(FOX_HEADS, tq, tk),
        compiler_params=pltpu.CompilerParams(dimension_semantics=("parallel", "arbitrary"),
                                             vmem_limit_bytes=VMEM_LIMIT),
        name="fox_attn",
    )(q, k_aug, vt)


def _dsa_kernel(dq_ref, iq_ref, w_ref, dk_ref, dvt_ref, ik_ref, o_ref,
                qm_ref, iqm_ref, key_ref, hi_ref, lo_ref, drop_ref, s_ref, st_ref, m_ref, acc_ref,
                *, tq, tk, q_pos0, n_real, top_k, nk):
    row0 = q_pos0 + pl.program_id(1) * tq
    q_pos = row0 + lax.broadcasted_iota(jnp.int32, (1, tq), 1)
    lim = jnp.minimum(lax.shift_left(jnp.right_shift(q_pos, CHUNK_SHIFT) + 1, CHUNK_SHIFT), n_real)
    lim_first = jnp.minimum((row0 // CHUNK + 1) * CHUNK, n_real)
    lim_last = jnp.minimum(((row0 + tq - 1) // CHUNK + 1) * CHUNK, n_real)
    n_full = lim_first // tk
    nt = (lim_last + tk - 1) // tk
    _mask_heads(dq_ref, qm_ref, DSA_HEADS)
    _mask_heads(iq_ref, iqm_ref, IDX_HEADS)

    def ktile(ref, j, lo, hi):
        return ref[0, pl.ds(pl.multiple_of(j * tk, tk), tk), lo:hi]

    def key_pos(j):
        return j * tk + lax.broadcasted_iota(jnp.int32, (tk, tq), 0)

    def score_tile(j, edge):
        ikt = ktile(ik_ref, j, 0, LANES)
        sc = jnp.zeros((tk, tq), F32)
        for h in range(IDX_HEADS):
            sc = sc + w_ref[0, h:h + 1, :] * jnp.maximum(_nt_dot(ikt, iqm_ref[h]), 0.0)
        bits = pltpu.bitcast(sc, jnp.int32)
        key = jnp.where(bits < 0, (bits ^ 0x7FFFFFFF) + 1 - ZERO_BAND, bits)
        pos = key_pos(j)
        key = jnp.where(sc == 0.0, -1 - pos, key)
        if edge:
            key = jnp.where(pos < lim, key, INT_MIN)
        key_ref[j] = key
        hi_ref[j] = jnp.right_shift(key, 16).astype(jnp.int16)
        lo_ref[j] = (jnp.bitwise_and(key, 0xFFFF) - 32768).astype(jnp.int16)

    def score_full(j, carry):
        score_tile(j, False)
        return carry

    def score_edge(j, carry):
        score_tile(j, True)
        return carry

    lax.fori_loop(0, n_full, score_full, 0)
    lax.fori_loop(n_full, nt, score_edge, 0)

    def count16(ref, cand):
        cand16 = cand.astype(jnp.int16)

        def body(j, acc):
            hit = jnp.where(ref[j] >= cand16, jnp.asarray(1, BF16), jnp.asarray(0, BF16))
            part = hit[0:16, :]
            for g in range(1, tk // 16):
                part = part + hit[g * 16:(g + 1) * 16, :]
            return acc + part.astype(F32)

        acc = lax.fori_loop(0, nt, body, jnp.zeros((16, tq), F32))
        return jnp.sum(acc, axis=0, keepdims=True)

    def bisect16(ref, base, cnt0):
        def step(it, carry):
            t, cnt_t = carry
            cand = t + lax.shift_left(jnp.int32(1), 15 - it)
            cnt = base + count16(ref, cand)
            ok = cnt >= top_k
            return jnp.where(ok, cand, t), jnp.where(ok, cnt, cnt_t)
        return lax.fori_loop(0, 16, step, (jnp.full((1, tq), -32768, jnp.int32), cnt0))

    t_hi, cnt_hi = bisect16(hi_ref, 0.0, jnp.full((1, tq), 2.0 ** 30, F32))
    above = jnp.where(t_hi >= 32767, 0.0, count16(hi_ref, jnp.minimum(t_hi + 1, 32767)))
    t_hi16 = t_hi.astype(jnp.int16)

    def keep_low(j, carry):
        lo_ref[j] = jnp.where(hi_ref[j] == t_hi16, lo_ref[j], jnp.asarray(-32768, jnp.int16))
        return carry

    lax.fori_loop(0, nt, keep_low, 0)
    t_lo, cnt_thr = bisect16(lo_ref, above, cnt_hi)
    thr = t_hi * 65536 + (t_lo + 32768)

    def count(pred):
        def body(j, acc):
            hit = jnp.where(pred(key_ref[j], j), 1.0, 0.0)
            return acc + jnp.sum(hit.reshape(tk // SUBLANES, SUBLANES, tq), axis=0)
        acc = lax.fori_loop(0, nt, body, jnp.zeros((SUBLANES, tq), F32))
        return jnp.sum(acc, axis=0, keepdims=True)

    excess = jnp.logical_and(cnt_thr > top_k, thr > INT_MIN)

    @pl.when(jnp.max(jnp.where(excess, 1.0, 0.0)) > 0.0)
    def _():
        need = top_k - count(lambda kt, j: kt > thr)
        bound = jnp.zeros((1, tq), jnp.int32)
        for bit in reversed(range(int(nk * tk - 1).bit_length())):
            cand = bound + (1 << bit)
            c = count(lambda kt, j: jnp.logical_and(kt == thr, key_pos(j) < cand))
            bound = jnp.where(c < need, cand, bound)

        def demote(j, carry):
            kt = key_ref[j]
            drop = jnp.logical_and(jnp.logical_and(kt == thr, key_pos(j) > bound), excess)
            key_ref[j] = jnp.where(drop, kt - 1, kt)
            return carry

        lax.fori_loop(0, nt, demote, 0)

    thr_sel = jnp.maximum(thr, INT_MIN + 1)
    _init_softmax(m_ref, acc_ref)
    kv_head = lambda h: h // (DSA_HEADS // DSA_KV_HEADS)

    def logits(j):
        drop_ref[...] = jnp.where(key_ref[j] >= thr_sel, 0.0, NEG_INF)
        for h in range(DSA_HEADS):
            g = kv_head(h)
            yield _nt_dot(ktile(dk_ref, j, g * LANES, (g + 1) * LANES), qm_ref[h]) + drop_ref[...]

    def values(j, h):
        return dvt_ref[0, j, kv_head(h) * VAL_ROWS:(kv_head(h) + 1) * VAL_ROWS, :]

    _flash_loop([(nt, logits)], logits, values, s_ref, st_ref, m_ref, acc_ref)
    _store_heads(o_ref, acc_ref, DSA_HEADS)


def _dsa_attn(dq, iq, w_t, dk2, dv, ik2, *, q_pos0, n_real, top_k, tq, tk):
    b, tq_all, _ = dq.shape
    tk_all = dk2.shape[1]
    assert tq_all % tq == 0 and tk_all % tk == 0 and tq % LANES == 0 and tk % 16 == 0 and tk_all <= ZERO_BAND
    nk = tk_all // tk
    dvt = _key_major(dv, tk)
    qspec = lambda w: pl.BlockSpec((1, tq, w), lambda bb, i: (bb, i, 0))
    kspec = lambda w: pl.BlockSpec((1, tk_all, w), lambda bb, i: (bb, 0, 0))
    return pl.pallas_call(
        functools.partial(_dsa_kernel, tq=tq, tk=tk, q_pos0=q_pos0, n_real=n_real, top_k=top_k, nk=nk),
        grid=(b, tq_all // tq),
        in_specs=[qspec(DSA_WIDTH), qspec(IDX_HEADS * HEAD_DIM),
                  pl.BlockSpec((1, SUBLANES, tq), lambda bb, i: (bb, 0, i)),
                  kspec(2 * LANES),
                  pl.BlockSpec((1, nk, DSA_KV_HEADS * VAL_ROWS, tk), lambda bb, i: (bb, 0, 0, 0)),
                  kspec(LANES)],
        out_specs=qspec(DSA_WIDTH),
        out_shape=jax.ShapeDtypeStruct(dq.shape, BF16),
        scratch_shapes=[pltpu.VMEM((DSA_HEADS, tq, LANES), BF16), pltpu.VMEM((IDX_HEADS, tq, LANES), BF16),
                        pltpu.VMEM((nk, tk, tq), jnp.int32), pltpu.VMEM((nk, tk, tq), jnp.int16),
                        pltpu.VMEM((nk, tk, tq), jnp.int16), pltpu.VMEM((tk, tq), F32)]
                       + _flash_scratch(DSA_HEADS, tq, tk),
        compiler_params=pltpu.CompilerParams(dimension_semantics=("parallel", "arbitrary"),
                                             vmem_limit_bytes=VMEM_LIMIT),
        name="dsa_attn",
    )(dq, iq, w_t, dk2, dvt, ik2)


def _layer_norm(x, g, b):
    mu = jnp.mean(x, axis=-1, keepdims=True)
    xc = x - mu
    var = jnp.mean(xc * xc, axis=-1, keepdims=True)
    return xc * lax.rsqrt(var + LN_EPS) * g + b


def _post_kernel(x_ref, fox_ref, dsa_ref, wo_ref, g1_ref, b1_ref, wg_ref, wu_ref, wd_ref, g2_ref, b2_ref,
                 o_ref, *, alpha, ff_chunk):
    mix = (jnp.dot(fox_ref[...], wo_ref[:FOX_WIDTH, :], preferred_element_type=F32)
           + jnp.dot(dsa_ref[...], wo_ref[FOX_WIDTH:, :], preferred_element_type=F32))
    h = _layer_norm(alpha * x_ref[...] + mix, g1_ref[...], b1_ref[...])
    hb = h.astype(BF16)
    f = jnp.zeros_like(h)
    for c in range(wg_ref.shape[1] // ff_chunk):
        lo, hi = c * ff_chunk, (c + 1) * ff_chunk
        gate = jnp.dot(hb, wg_ref[:, lo:hi], preferred_element_type=F32)
        up = jnp.dot(hb, wu_ref[:, lo:hi], preferred_element_type=F32)
        act = (gate * jax.nn.sigmoid(gate) * up).astype(BF16)
        f = f + jnp.dot(act, wd_ref[lo:hi, :], preferred_element_type=F32)
    o_ref[...] = _layer_norm(alpha * h + f, g2_ref[...], b2_ref[...])


def _post(x2, fox_o, dsa_o, w_out, ln1_g, ln1_b, w_gate, w_up, w_down, ln2_g, ln2_b, *, alpha, tm, ff_chunk=256):
    n, d = x2.shape
    d_ff = w_gate.shape[1]
    assert n % tm == 0 and d_ff % ff_chunk == 0
    row = lambda t: (t, 0)
    const = lambda t: (0, 0)
    wspec = lambda shape: pl.BlockSpec(shape, const, pipeline_mode=pl.Buffered(1))
    vec = lambda a: a.astype(F32)[None, :]
    return pl.pallas_call(
        functools.partial(_post_kernel, alpha=alpha, ff_chunk=ff_chunk),
        grid=(n // tm,),
        in_specs=[pl.BlockSpec((tm, d), row), pl.BlockSpec((tm, FOX_WIDTH), row), pl.BlockSpec((tm, DSA_WIDTH), row),
                  wspec(w_out.shape), wspec((1, d)), wspec((1, d)),
                  wspec(w_gate.shape), wspec(w_up.shape), wspec(w_down.shape), wspec((1, d)), wspec((1, d))],
        out_specs=pl.BlockSpec((tm, d), row),
        out_shape=jax.ShapeDtypeStruct((n, d), F32),
        compiler_params=pltpu.CompilerParams(dimension_semantics=("parallel",), vmem_limit_bytes=VMEM_LIMIT),
        name="post",
    )(x2, fox_o, dsa_o, w_out.astype(BF16), vec(ln1_g), vec(ln1_b),
      w_gate.astype(BF16), w_up.astype(BF16), w_down.astype(BF16), vec(ln2_g), vec(ln2_b))


def _round_up(n, m):
    return -(-n // m) * m


def _pad_rows(a, rows):
    return jnp.pad(a, ((0, 0), (0, rows - a.shape[1]), (0, 0)))


def _largest_tile(n, candidates):
    return next(c for c in candidates if n % c == 0)


def _mixer_layer(y, past, weights, *, alpha):
    (w_packed, bias, w_out, ln1_g, ln1_b, w_gate, w_up, w_down, ln2_g, ln2_b) = weights
    b, t, d = y.shape
    p_len = 0 if past is None else past[0].shape[1]
    tables = _rope_lane_tables(p_len + jnp.arange(t))
    (fq, fk, fkb, fv, fvb, dq, dk, dv, dk2, dvb, iq, ik, ik2, misc) = _project(
        y.reshape(b * t, d), w_packed, bias, tables, t, min(512, t))
    r3 = lambda a: a.reshape(b, t, a.shape[-1])
    fq, fkb, fvb, dq, dk2, dvb, iq, ik2, misc3 = map(r3, (fq, fkb, fvb, dq, dk2, dvb, iq, ik2, misc))

    t_pad = _round_up(t, LANES)
    neg_c_new = _cumsum(_pad_rows(misc3, t_pad), scale=-LOG2E)[:, :t]
    n_keys = p_len + t
    top_k = min(IDX_TOPK_MAX, n_keys // 4)
    if past is None:
        n_pad = n_keys
        k_all, v_all, kb = fkb, fvb, neg_c_new
        dk_all, dv_all, ik_all = dk2, dvb, ik2
    else:
        cf_k, cf_v, cf_logf, cd_k, cd_v, ci_k = past
        n_pad = _round_up(n_keys, LANES)
        cat = lambda old, new: _pad_rows(jnp.concatenate([old.astype(new.dtype), new], axis=1), n_pad)
        k_all = cat(cf_k.reshape(b, p_len, FOX_WIDTH), fkb)
        v_all = cat(cf_v.reshape(b, p_len, FOX_WIDTH), fvb)
        dk_all = cat(jnp.repeat(cd_k, 2, axis=2).reshape(b, p_len, 2 * LANES), dk2)
        dv_all = cat(cd_v.reshape(b, p_len, LANES), dvb)
        ik_all = cat(jnp.concatenate([ci_k, ci_k], axis=-1), ik2)
        lf_past = jnp.pad(cf_logf.astype(F32), ((0, 0), (0, 0), (0, LANES - FOX_HEADS)))
        d_past = _cumsum(lf_past, reverse=True, inclusive=False, scale=LOG2E)
        kb = cat(d_past, neg_c_new)
    tq = min(512, t_pad)
    tk = _largest_tile(n_pad, (512, 384, 256, 128))
    pad_q = lambda a: _pad_rows(a, t_pad)
    w_t = jnp.transpose(misc3[:, :, _MISC_IW:_MISC_IW + SUBLANES], (0, 2, 1))
    w_t = jnp.pad(w_t, ((0, 0), (0, 0), (0, t_pad - t)))
    fox_o = _fox_attn(pad_q(fq), k_all, v_all, kb, q_pos0=p_len, tq=tq, tk=tk)[:, :t]
    dsa_o = _dsa_attn(pad_q(dq), pad_q(iq), w_t, dk_all, dv_all, ik_all, q_pos0=p_len, n_real=n_keys,
                      top_k=top_k, tq=tq, tk=tk)[:, :t]
    y_out = _post(y.reshape(b * t, d), fox_o.reshape(b * t, -1), dsa_o.reshape(b * t, -1),
                  w_out, ln1_g, ln1_b, w_gate, w_up, w_down, ln2_g, ln2_b, alpha=alpha, tm=min(512, b * t))
    states = (fk.reshape(b, t, FOX_HEADS, HEAD_DIM), fv.reshape(b, t, FOX_HEADS, HEAD_DIM),
              misc3[:, :, :FOX_HEADS],
              dk.reshape(b, t, DSA_KV_HEADS, HEAD_DIM), dv.reshape(b, t, DSA_KV_HEADS, HEAD_DIM),
              ik.reshape(b, t, HEAD_DIM))
    return y_out.reshape(b, t, d), states


def kernel(x_prompt, x_sample, cache_fox_k, cache_fox_v, cache_fox_logf, cache_dsa_k, cache_dsa_v, cache_idx_k,
           w_in, b_forget, w_out, ln1_g, ln1_b, w_gate, w_up, w_down, ln2_g, ln2_b):
    depth = w_in.shape[0]
    alpha = (2 * depth) ** 0.25
    yp, ys = x_prompt, x_sample
    p_states, s_states = [], []
    for l in range(depth):
        w_packed, bias = _pack_w_in(w_in[l], b_forget[l])
        weights = (w_packed, bias, w_out[l], ln1_g[l], ln1_b[l], w_gate[l], w_up[l], w_down[l], ln2_g[l], ln2_b[l])
        yp, st = _mixer_layer(yp, None, weights, alpha=alpha)
        p_states.append(st)
        past = (cache_fox_k[l], cache_fox_v[l], cache_fox_logf[l], cache_dsa_k[l], cache_dsa_v[l], cache_idx_k[l])
        ys, st = _mixer_layer(ys, past, weights, alpha=alpha)
        s_states.append(st)
    stack = lambda sts: tuple(jnp.stack([s[n] for s in sts], axis=0) for n in range(6))
    return (yp, ys) + stack(p_states) + stack(s_states)
```

```python
import functools

import numpy as np
import jax
import jax.numpy as jnp
from jax import lax
from jax.experimental import pallas as pl
from jax.experimental.pallas import tpu as pltpu

HEAD_DIM = 64
FOX_HEADS = 8
DSA_HEADS = 8
DSA_KV_HEADS = 2
IDX_HEADS = 4
CHUNK = 64
CHUNK_SHIFT = 6
IDX_TOPK_MAX = 256
ROPE_THETA = 500000.0
ROT_DIM = HEAD_DIM // 4
LN_EPS = 1e-5
LOG2E = 1.4426950408889634

LANES = 128
SUBLANES = 8
FOX_WIDTH = FOX_HEADS * HEAD_DIM
DSA_WIDTH = DSA_HEADS * HEAD_DIM
INT_MIN = -(2 ** 31)
ZERO_BAND = 1 << 13
NEG_INF = float("-inf")
F32 = jnp.float32
BF16 = jnp.bfloat16
VMEM_LIMIT = 56 * 1024 * 1024

_C_FQ, _C_FK, _C_FV, _C_DQ = 0, 512, 1024, 1536
_C_DK, _C_DV, _C_IQ, _C_IK, _C_MISC, _C_END = 2048, 2176, 2304, 2560, 2688, 2816
_MISC_IW = 8


def _nt_dot(a, b):
    return lax.dot_general(a, b, (((1,), (1,)), ((), ())), preferred_element_type=F32)


def _half_mask(shape, hh):
    lane = lax.broadcasted_iota(jnp.int32, shape, len(shape) - 1)
    return (lane < HEAD_DIM) if hh == 0 else (lane >= HEAD_DIM)


def _project_kernel(x_ref, w_ref, bias_ref, c_ref, sa_ref, sb_ref,
                    fq_ref, fk_ref, fkb_ref, fv_ref, fvb_ref, dq_ref, dk_ref, dv_ref,
                    dk2_ref, dvb_ref, iq_ref, ik_ref, ik2_ref, misc_ref):
    xb = x_ref[...].astype(BF16)
    cos, sa, sb = c_ref[...], sa_ref[...], sb_ref[...]

    def mm(lo, hi):
        return jnp.dot(xb, w_ref[:, lo:hi], preferred_element_type=F32)

    def rope(yb):
        return yb * cos + pltpu.roll(yb, LANES - ROT_DIM // 2, 1) * sa + pltpu.roll(yb, ROT_DIM // 2, 1) * sb

    def blocks(y):
        return [y[:, b * LANES:(b + 1) * LANES] for b in range(y.shape[1] // LANES)]

    fq_ref[...] = mm(_C_FQ, _C_FK).astype(BF16)
    y = mm(_C_FK, _C_FV)
    fk_ref[...] = y
    fkb_ref[...] = y.astype(BF16)
    y = mm(_C_FV, _C_DQ)
    fv_ref[...] = y
    fvb_ref[...] = y.astype(BF16)
    for b, yb in enumerate(blocks(mm(_C_DQ, _C_DK))):
        dq_ref[:, b * LANES:(b + 1) * LANES] = rope(yb).astype(BF16)

    half0 = _half_mask((x_ref.shape[0], LANES), 0)
    kv = mm(_C_DK, _C_IQ)
    k = rope(kv[:, :LANES])
    v = kv[:, LANES:]
    for src, dst in ((k, dk_ref), (v, dv_ref)):
        for g in range(DSA_KV_HEADS):
            dst[:, g, :] = src[:, g * HEAD_DIM:(g + 1) * HEAD_DIM]
    dvb_ref[...] = v.astype(BF16)
    sw = pltpu.roll(k, HEAD_DIM, 1)
    dk2_ref[:, :LANES] = jnp.where(half0, k, sw).astype(BF16)
    dk2_ref[:, LANES:] = jnp.where(half0, sw, k).astype(BF16)

    for b, yb in enumerate(blocks(mm(_C_IQ, _C_IK))):
        iq_ref[:, b * LANES:(b + 1) * LANES] = rope(yb).astype(BF16)
    ik = rope(mm(_C_IK, _C_MISC))
    ik_ref[...] = ik[:, :HEAD_DIM]
    ik2_ref[...] = ik.astype(BF16)

    z = mm(_C_MISC, _C_END)
    zf = z + bias_ref[...]
    logf = jnp.minimum(zf, 0.0) - jnp.log1p(jnp.exp(-jnp.abs(zf)))
    lane = lax.broadcasted_iota(jnp.int32, z.shape, 1)
    misc_ref[...] = jnp.where(lane < _MISC_IW, logf, z)


def _pack_w_in(w, b_forget):
    d = w.shape[0]
    sizes = (FOX_WIDTH, FOX_WIDTH, FOX_WIDTH, FOX_HEADS, DSA_WIDTH, DSA_KV_HEADS * HEAD_DIM,
             DSA_KV_HEADS * HEAD_DIM, IDX_HEADS * HEAD_DIM, HEAD_DIM, IDX_HEADS)
    offs = np.concatenate([[0], np.cumsum(sizes)])
    fq, fk, fv, ff, dq, dk, dv, iq, ik, iw = [w[:, offs[i]:offs[i + 1]] for i in range(len(sizes))]
    scale = HEAD_DIM ** -0.5 * LOG2E
    misc = jnp.concatenate([ff, iw, jnp.zeros((d, LANES - FOX_HEADS - IDX_HEADS), w.dtype)], axis=1)
    packed = jnp.concatenate([fq * scale, fk, fv, dq * scale, dk, dv, iq, ik, ik, misc], axis=1)
    bias = jnp.concatenate([b_forget.astype(F32), jnp.zeros((LANES - FOX_HEADS,), F32)])[None, :]
    return packed.astype(BF16), bias


def _rope_lane_tables(positions):
    half = ROT_DIM // 2
    inv_freq = ROPE_THETA ** (-jnp.arange(half, dtype=F32) * 2.0 / ROT_DIM)
    ang = positions.astype(F32)[:, None] * inv_freq[None, :]
    cos, sin = jnp.cos(ang), jnp.sin(ang)
    n = positions.shape[0]
    rest = HEAD_DIM - ROT_DIM
    c = jnp.concatenate([cos, cos, jnp.ones((n, rest), F32)], axis=1)
    sa = jnp.concatenate([-sin, jnp.zeros((n, half + rest), F32)], axis=1)
    sb = jnp.concatenate([jnp.zeros((n, half), F32), sin, jnp.zeros((n, rest), F32)], axis=1)
    rep = LANES // HEAD_DIM
    return tuple(jnp.tile(t, (1, rep)) for t in (c, sa, sb))


def _project(x2, w_packed, bias, tables, seq, tm):
    n, d = x2.shape
    assert seq % tm == 0 and n % seq == 0
    tiles_per_seq = seq // tm
    row = lambda t: (t, 0)
    pos = lambda t: (t % tiles_per_seq, 0)
    const = lambda t: (0, 0)

    def out(width, dtype):
        return jax.ShapeDtypeStruct((n, width), dtype), pl.BlockSpec((tm, width), row)

    kv_state = (jax.ShapeDtypeStruct((n, DSA_KV_HEADS, HEAD_DIM), F32),
                pl.BlockSpec((tm, DSA_KV_HEADS, HEAD_DIM), lambda t: (t, 0, 0)))
    outs = [out(FOX_WIDTH, BF16),
            out(FOX_WIDTH, F32), out(FOX_WIDTH, BF16),
            out(FOX_WIDTH, F32), out(FOX_WIDTH, BF16),
            out(DSA_WIDTH, BF16),
            kv_state, kv_state,
            out(2 * LANES, BF16), out(LANES, BF16),
            out(IDX_HEADS * HEAD_DIM, BF16),
            out(HEAD_DIM, F32), out(LANES, BF16),
            out(LANES, F32)]
    return pl.pallas_call(
        _project_kernel,
        grid=(n // tm,),
        in_specs=[pl.BlockSpec((tm, d), row),
                  pl.BlockSpec((d, _C_END), const, pipeline_mode=pl.Buffered(1)),
                  pl.BlockSpec((1, LANES), const),
                  pl.BlockSpec((tm, LANES), pos), pl.BlockSpec((tm, LANES), pos), pl.BlockSpec((tm, LANES), pos)],
        out_specs=[o[1] for o in outs],
        out_shape=[o[0] for o in outs],
        compiler_params=pltpu.CompilerParams(dimension_semantics=("parallel",), vmem_limit_bytes=VMEM_LIMIT),
        name="project",
    )(x2, w_packed, bias, *tables)


def _cumsum_kernel(x_ref, o_ref, carry_ref, *, reverse, inclusive, scale):
    @pl.when(pl.program_id(1) == 0)
    def _():
        carry_ref[...] = jnp.zeros_like(carry_ref)

    x = x_ref[0]
    tm = x.shape[0]
    hi = x.astype(BF16)
    r1 = x - hi.astype(F32)
    mid = r1.astype(BF16)
    lo = (r1 - mid.astype(F32)).astype(BF16)
    row = lax.broadcasted_iota(jnp.int32, (tm, tm), 0)
    col = lax.broadcasted_iota(jnp.int32, (tm, tm), 1)
    if reverse:
        keep = (col >= row) if inclusive else (col > row)
    else:
        keep = (col <= row) if inclusive else (col < row)
    tri = jnp.where(keep, 1.0, 0.0).astype(BF16)
    cs = (jnp.dot(tri, hi, preferred_element_type=F32) + jnp.dot(tri, mid, preferred_element_type=F32)
          + jnp.dot(tri, lo, preferred_element_type=F32))
    carry = carry_ref[0:1, :]
    o_ref[0] = scale * (cs + carry)
    carry_ref[0:1, :] = carry + jnp.sum(x, axis=0, keepdims=True)


def _cumsum(x, *, reverse=False, inclusive=True, scale=1.0, tm=512):
    b, s, w = x.shape
    tm = min(tm, s)
    assert s % tm == 0 and w == LANES
    nt = s // tm
    idx = (lambda i, t: (i, nt - 1 - t, 0)) if reverse else (lambda i, t: (i, t, 0))
    return pl.pallas_call(
        functools.partial(_cumsum_kernel, reverse=reverse, inclusive=inclusive, scale=scale),
        grid=(b, nt),
        in_specs=[pl.BlockSpec((1, tm, w), idx)],
        out_specs=pl.BlockSpec((1, tm, w), idx),
        out_shape=jax.ShapeDtypeStruct(x.shape, F32),
        scratch_shapes=[pltpu.VMEM((SUBLANES, w), F32)],
        compiler_params=pltpu.CompilerParams(dimension_semantics=("parallel", "arbitrary")),
        name="cumsum",
    )(x)


BIAS_PIECES = 3


def _bias_lane0(h):
    return HEAD_DIM if h % 2 == 0 else 0


def _mask_heads(q_ref, qm_ref, n_heads, bias_ones=False):
    for h in range(n_heads):
        qpair = q_ref[0, :, (h // 2) * LANES:(h // 2 + 1) * LANES]
        other = jnp.zeros_like(qpair)
        if bias_ones:
            lane = lax.broadcasted_iota(jnp.int32, qpair.shape, 1)
            in_bias = jnp.logical_and(lane >= _bias_lane0(h), lane < _bias_lane0(h) + BIAS_PIECES)
            other = jnp.where(in_bias, 1.0, 0.0).astype(qpair.dtype)
        qm_ref[h] = jnp.where(_half_mask(qpair.shape, h % 2), qpair, other)


ONES_ROWS = 16
VAL_ROWS = HEAD_DIM + ONES_ROWS


def _flash_scratch(n_heads, tq, tk):
    return [pltpu.VMEM((2, n_heads, tk, tq), F32), pltpu.VMEM((2, n_heads, SUBLANES, tq), F32),
            pltpu.VMEM((n_heads, 1, tq), F32), pltpu.VMEM((n_heads, VAL_ROWS, tq), F32)]


def _init_softmax(m_ref, acc_ref):
    m_ref[...] = jnp.full_like(m_ref, NEG_INF)
    acc_ref[...] = jnp.zeros_like(acc_ref)


def _flash_loop(segments, logits0, values, s_ref, st_ref, m_ref, acc_ref):
    def by_parity(j, fn):
        for slot in range(2):
            pl.when(lax.rem(j, 2) == slot)(functools.partial(fn, slot))

    n_heads = m_ref.shape[0]

    def stage_a(h, s, slot):
        s_ref[slot, h] = s
        m_prev = m_ref[h]
        m_new = jnp.maximum(m_prev, jnp.max(s, axis=0, keepdims=True))
        m_use = jnp.where(m_new == NEG_INF, 0.0, m_new)
        st_ref[slot, h, 0:1, :] = m_use
        st_ref[slot, h, 1:2, :] = jnp.exp2(m_prev - m_use)
        m_ref[h] = m_new

    def stage_b(j, h, slot):
        p = jnp.exp2(s_ref[slot, h] - st_ref[slot, h, 0:1, :]).astype(BF16)
        alpha = st_ref[slot, h, 1:2, :]
        acc_ref[h] = alpha * acc_ref[h] + jnp.dot(values(j, h), p, preferred_element_type=F32)

    for h, s in enumerate(logits0(0)):
        stage_a(h, s, 0)
    start = 1
    for end, logits in segments:
        def both(slot, j, logits):
            tiles = logits(j)
            for h in range(n_heads):
                stage_b(j - 1, h, 1 - slot)
                stage_a(h, next(tiles), slot)

        def body(j, carry, logits=logits):
            by_parity(j, functools.partial(both, j=j, logits=logits))
            return carry

        lax.fori_loop(start, end, body, 0)
        start = jnp.maximum(start, end)
    def drain(slot):
        for h in range(n_heads):
            stage_b(start - 1, h, slot)

    by_parity(start - 1, drain)


def _store_heads(o_ref, acc_ref, n_heads):
    def head(h):
        return acc_ref[h, :HEAD_DIM, :] / acc_ref[h, HEAD_DIM:HEAD_DIM + 1, :]

    for c in range(n_heads // 2):
        ot = jnp.concatenate([head(2 * c), head(2 * c + 1)], axis=0)
        o_ref[0, :, c * LANES:(c + 1) * LANES] = ot.T.astype(o_ref.dtype)


def _key_major(v, tk):
    b, t, c = v.shape
    heads = c // HEAD_DIM
    vt = jnp.transpose(v.reshape(b, t // tk, tk, heads, HEAD_DIM), (0, 1, 3, 4, 2))
    ones = jnp.ones((b, t // tk, heads, ONES_ROWS, tk), v.dtype)
    return jnp.concatenate([vt, ones], axis=3).reshape(b, t // tk, heads * VAL_ROWS, tk)


def _fox_keys_kernel(k_ref, kb_ref, o_ref):
    kb = kb_ref[0]
    hi = kb.astype(BF16).astype(F32)
    r1 = kb - hi
    mid = r1.astype(BF16).astype(F32)
    pieces = (hi, mid, r1 - mid)
    lane = lax.broadcasted_iota(jnp.int32, kb.shape, 1)
    for h in range(FOX_HEADS):
        bias = jnp.zeros_like(kb)
        for n, piece in enumerate(pieces):
            dst = _bias_lane0(h) + n
            bias = jnp.where(lane == dst, pltpu.roll(piece, (dst - h) % LANES, 1), bias)
        kpair = k_ref[0, :, (h // 2) * LANES:(h // 2 + 1) * LANES]
        o_ref[0, :, h * LANES:(h + 1) * LANES] = jnp.where(_half_mask(kpair.shape, h % 2), kpair, bias.astype(BF16))


def _fox_keys(k, kb, tm=512):
    b, t, _ = k.shape
    tm = _largest_tile(t, (tm, 384, 256, 128))
    spec = lambda w: pl.BlockSpec((1, tm, w), lambda bb, i: (bb, i, 0))
    return pl.pallas_call(
        _fox_keys_kernel,
        grid=(b, t // tm),
        in_specs=[spec(FOX_WIDTH), spec(LANES)],
        out_specs=spec(FOX_HEADS * LANES),
        out_shape=jax.ShapeDtypeStruct((b, t, FOX_HEADS * LANES), BF16),
        compiler_params=pltpu.CompilerParams(dimension_semantics=("parallel", "parallel")),
        name="fox_keys",
    )(k, kb)


def _fox_kernel(q_ref, k_ref, vt_ref, o_ref, qm_ref, s_ref, st_ref, m_ref, acc_ref, *, tq, tk, q_pos0, nk):
    row0 = q_pos0 + pl.program_id(1) * tq
    n_full = jnp.minimum(nk, (row0 + 1) // tk)
    j_end = jnp.minimum(nk, (row0 + tq - 1) // tk + 1)
    _mask_heads(q_ref, qm_ref, FOX_HEADS, bias_ones=True)
    _init_softmax(m_ref, acc_ref)

    def logits(j, masked):
        off = pl.multiple_of(j * tk, tk)
        if masked:
            key_pos = j * tk + lax.broadcasted_iota(jnp.int32, (tk, tq), 0)
            visible = key_pos <= row0 + lax.broadcasted_iota(jnp.int32, (tk, tq), 1)
        for h in range(FOX_HEADS):
            s = _nt_dot(k_ref[0, pl.ds(off, tk), h * LANES:(h + 1) * LANES], qm_ref[h])
            yield jnp.where(visible, s, NEG_INF) if masked else s

    def values(j, h):
        return vt_ref[0, j, h * VAL_ROWS:(h + 1) * VAL_ROWS, :]

    masked = functools.partial(logits, masked=True)
    _flash_loop([(n_full, functools.partial(logits, masked=False)), (j_end, masked)], masked, values,
                s_ref, st_ref, m_ref, acc_ref)
    _store_heads(o_ref, acc_ref, FOX_HEADS)


def _fox_attn(q, k, v, kb, *, q_pos0, tq, tk):
    b, tq_all, _ = q.shape
    tk_all = k.shape[1]
    assert tq_all % tq == 0 and tk_all % tk == 0 and tq % LANES == 0
    nk = tk_all // tk
    vt = _key_major(v, tk)
    k_aug = _fox_keys(k, kb)
    return pl.pallas_call(
        functools.partial(_fox_kernel, tq=tq, tk=tk, q_pos0=q_pos0, nk=nk),
        grid=(b, tq_all // tq),
        in_specs=[pl.BlockSpec((1, tq, FOX_WIDTH), lambda bb, i: (bb, i, 0)),
                  pl.BlockSpec((1, tk_all, FOX_HEADS * LANES), lambda bb, i: (bb, 0, 0)),
                  pl.BlockSpec((1, nk, FOX_HEADS * VAL_ROWS, tk), lambda bb, i: (bb, 0, 0, 0))],
        out_specs=pl.BlockSpec((1, tq, FOX_WIDTH), lambda bb, i: (bb, i, 0)),
        out_shape=jax.ShapeDtypeStruct(q.shape, BF16),
        scratch_shapes=[pltpu.VMEM((FOX_HEADS, tq, LANES), BF16)] + _flash_scratch(FOX_HEADS, tq, tk),
        compiler_params=pltpu.CompilerParams(dimension_semantics=("parallel", "arbitrary"),
                                             vmem_limit_bytes=VMEM_LIMIT),
        name="fox_attn",
    )(q, k_aug, vt)


def _dsa_kernel(dq_ref, iq_ref, w_ref, dk_ref, dvt_ref, ik_ref, o_ref,
                qm_ref, iqm_ref, key_ref, hi_ref, lo_ref, drop_ref, s_ref, st_ref, m_ref, acc_ref,
                *, tq, tk, q_pos0, n_real, top_k, nk):
    row0 = q_pos0 + pl.program_id(1) * tq
    q_pos = row0 + lax.broadcasted_iota(jnp.int32, (1, tq), 1)
    lim = jnp.minimum(lax.shift_left(jnp.right_shift(q_pos, CHUNK_SHIFT) + 1, CHUNK_SHIFT), n_real)
    lim_first = jnp.minimum((row0 // CHUNK + 1) * CHUNK, n_real)
    lim_last = jnp.minimum(((row0 + tq - 1) // CHUNK + 1) * CHUNK, n_real)
    n_full = lim_first // tk
    nt = (lim_last + tk - 1) // tk
    _mask_heads(dq_ref, qm_ref, DSA_HEADS)
    _mask_heads(iq_ref, iqm_ref, IDX_HEADS)

    def ktile(ref, j, lo, hi):
        return ref[0, pl.ds(pl.multiple_of(j * tk, tk), tk), lo:hi]

    def key_pos(j):
        return j * tk + lax.broadcasted_iota(jnp.int32, (tk, tq), 0)

    def score_tile(j, edge):
        ikt = ktile(ik_ref, j, 0, LANES)
        sc = jnp.zeros((tk, tq), F32)
        for h in range(IDX_HEADS):
            sc = sc + w_ref[0, h:h + 1, :] * jnp.maximum(_nt_dot(ikt, iqm_ref[h]), 0.0)
        bits = pltpu.bitcast(sc, jnp.int32)
        key = jnp.where(bits < 0, (bits ^ 0x7FFFFFFF) + 1 - ZERO_BAND, bits)
        pos = key_pos(j)
        key = jnp.where(sc == 0.0, -1 - pos, key)
        if edge:
            key = jnp.where(pos < lim, key, INT_MIN)
        key_ref[j] = key
        hi_ref[j] = jnp.right_shift(key, 16).astype(jnp.int16)
        lo_ref[j] = (jnp.bitwise_and(key, 0xFFFF) - 32768).astype(jnp.int16)

    def score_full(j, carry):
        score_tile(j, False)
        return carry

    def score_edge(j, carry):
        score_tile(j, True)
        return carry

    lax.fori_loop(0, n_full, score_full, 0)
    lax.fori_loop(n_full, nt, score_edge, 0)

    def count16(ref, cand):
        cand16 = cand.astype(jnp.int16)

        def body(j, acc):
            hit = jnp.where(ref[j] >= cand16, jnp.asarray(1, BF16), jnp.asarray(0, BF16))
            part = hit[0:16, :]
            for g in range(1, tk // 16):
                part = part + hit[g * 16:(g + 1) * 16, :]
            return acc + part.astype(F32)

        acc = lax.fori_loop(0, nt, body, jnp.zeros((16, tq), F32))
        return jnp.sum(acc, axis=0, keepdims=True)

    def bisect16(ref, base, cnt0):
        def step(it, carry):
            t, cnt_t = carry
            cand = t + lax.shift_left(jnp.int32(1), 15 - it)
            cnt = base + count16(ref, cand)
            ok = cnt >= top_k
            return jnp.where(ok, cand, t), jnp.where(ok, cnt, cnt_t)
        return lax.fori_loop(0, 16, step, (jnp.full((1, tq), -32768, jnp.int32), cnt0))

    t_hi, cnt_hi = bisect16(hi_ref, 0.0, jnp.full((1, tq), 2.0 ** 30, F32))
    above = jnp.where(t_hi >= 32767, 0.0, count16(hi_ref, jnp.minimum(t_hi + 1, 32767)))
    t_hi16 = t_hi.astype(jnp.int16)

    def keep_low(j, carry):
        lo_ref[j] = jnp.where(hi_ref[j] == t_hi16, lo_ref[j], jnp.asarray(-32768, jnp.int16))
        return carry

    lax.fori_loop(0, nt, keep_low, 0)
    t_lo, cnt_thr = bisect16(lo_ref, above, cnt_hi)
    thr = t_hi * 65536 + (t_lo + 32768)

    def count(pred):
        def body(j, acc):
            hit = jnp.where(pred(key_ref[j], j), 1.0, 0.0)
            return acc + jnp.sum(hit.reshape(tk // SUBLANES, SUBLANES, tq), axis=0)
        acc = lax.fori_loop(0, nt, body, jnp.zeros((SUBLANES, tq), F32))
        return jnp.sum(acc, axis=0, keepdims=True)

    excess = jnp.logical_and(cnt_thr > top_k, thr > INT_MIN)

    @pl.when(jnp.max(jnp.where(excess, 1.0, 0.0)) > 0.0)
    def _():
        need = top_k - count(lambda kt, j: kt > thr)
        bound = jnp.zeros((1, tq), jnp.int32)
        for bit in reversed(range(int(nk * tk - 1).bit_length())):
            cand = bound + (1 << bit)
            c = count(lambda kt, j: jnp.logical_and(kt == thr, key_pos(j) < cand))
            bound = jnp.where(c < need, cand, bound)

        def demote(j, carry):
            kt = key_ref[j]
            drop = jnp.logical_and(jnp.logical_and(kt == thr, key_pos(j) > bound), excess)
            key_ref[j] = jnp.where(drop, kt - 1, kt)
            return carry

        lax.fori_loop(0, nt, demote, 0)

    thr_sel = jnp.maximum(thr, INT_MIN + 1)
    _init_softmax(m_ref, acc_ref)
    kv_head = lambda h: h // (DSA_HEADS // DSA_KV_HEADS)

    def logits(j):
        drop_ref[...] = jnp.where(key_ref[j] >= thr_sel, 0.0, NEG_INF)
        for h in range(DSA_HEADS):
            g = kv_head(h)
            yield _nt_dot(ktile(dk_ref, j, g * LANES, (g + 1) * LANES), qm_ref[h]) + drop_ref[...]

    def values(j, h):
        return dvt_ref[0, j, kv_head(h) * VAL_ROWS:(kv_head(h) + 1) * VAL_ROWS, :]

    _flash_loop([(nt, logits)], logits, values, s_ref, st_ref, m_ref, acc_ref)
    _store_heads(o_ref, acc_ref, DSA_HEADS)


def _dsa_attn(dq, iq, w_t, dk2, dv, ik2, *, q_pos0, n_real, top_k, tq, tk):
    b, tq_all, _ = dq.shape
    tk_all = dk2.shape[1]
    assert tq_all % tq == 0 and tk_all % tk == 0 and tq % LANES == 0 and tk % 16 == 0 and tk_all <= ZERO_BAND
    nk = tk_all // tk
    dvt = _key_major(dv, tk)
    qspec = lambda w: pl.BlockSpec((1, tq, w), lambda bb, i: (bb, i, 0))
    kspec = lambda w: pl.BlockSpec((1, tk_all, w), lambda bb, i: (bb, 0, 0))
    return pl.pallas_call(
        functools.partial(_dsa_kernel, tq=tq, tk=tk, q_pos0=q_pos0, n_real=n_real, top_k=top_k, nk=nk),
        grid=(b, tq_all // tq),
        in_specs=[qspec(DSA_WIDTH), qspec(IDX_HEADS * HEAD_DIM),
                  pl.BlockSpec((1, SUBLANES, tq), lambda bb, i: (bb, 0, i)),
                  kspec(2 * LANES),
                  pl.BlockSpec((1, nk, DSA_KV_HEADS * VAL_ROWS, tk), lambda bb, i: (bb, 0, 0, 0)),
                  kspec(LANES)],
        out_specs=qspec(DSA_WIDTH),
        out_shape=jax.ShapeDtypeStruct(dq.shape, BF16),
        scratch_shapes=[pltpu.VMEM((DSA_HEADS, tq, LANES), BF16), pltpu.VMEM((IDX_HEADS, tq, LANES), BF16),
                        pltpu.VMEM((nk, tk, tq), jnp.int32), pltpu.VMEM((nk, tk, tq), jnp.int16),
                        pltpu.VMEM((nk, tk, tq), jnp.int16), pltpu.VMEM((tk, tq), F32)]
                       + _flash_scratch(DSA_HEADS, tq, tk),
        compiler_params=pltpu.CompilerParams(dimension_semantics=("parallel", "arbitrary"),
                                             vmem_limit_bytes=VMEM_LIMIT),
        name="dsa_attn",
    )(dq, iq, w_t, dk2, dvt, ik2)


def _layer_norm(x, g, b):
    mu = jnp.mean(x, axis=-1, keepdims=True)
    xc = x - mu
    var = jnp.mean(xc * xc, axis=-1, keepdims=True)
    return xc * lax.rsqrt(var + LN_EPS) * g + b


def _post_kernel(x_ref, fox_ref, dsa_ref, wo_ref, g1_ref, b1_ref, wg_ref, wu_ref, wd_ref, g2_ref, b2_ref,
                 o_ref, *, alpha, ff_chunk):
    mix = (jnp.dot(fox_ref[...], wo_ref[:FOX_WIDTH, :], preferred_element_type=F32)
           + jnp.dot(dsa_ref[...], wo_ref[FOX_WIDTH:, :], preferred_element_type=F32))
    h = _layer_norm(alpha * x_ref[...] + mix, g1_ref[...], b1_ref[...])
    hb = h.astype(BF16)
    f = jnp.zeros_like(h)
    for c in range(wg_ref.shape[1] // ff_chunk):
        lo, hi = c * ff_chunk, (c + 1) * ff_chunk
        gate = jnp.dot(hb, wg_ref[:, lo:hi], preferred_element_type=F32)
        up = jnp.dot(hb, wu_ref[:, lo:hi], preferred_element_type=F32)
        act = (gate * jax.nn.sigmoid(gate) * up).astype(BF16)
        f = f + jnp.dot(act, wd_ref[lo:hi, :], preferred_element_type=F32)
    o_ref[...] = _layer_norm(alpha * h + f, g2_ref[...], b2_ref[...])


def _post(x2, fox_o, dsa_o, w_out, ln1_g, ln1_b, w_gate, w_up, w_down, ln2_g, ln2_b, *, alpha, tm, ff_chunk=256):
    n, d = x2.shape
    d_ff = w_gate.shape[1]
    assert n % tm == 0 and d_ff % ff_chunk == 0
    row = lambda t: (t, 0)
    const = lambda t: (0, 0)
    wspec = lambda shape: pl.BlockSpec(shape, const, pipeline_mode=pl.Buffered(1))
    vec = lambda a: a.astype(F32)[None, :]
    return pl.pallas_call(
        functools.partial(_post_kernel, alpha=alpha, ff_chunk=ff_chunk),
        grid=(n // tm,),
        in_specs=[pl.BlockSpec((tm, d), row), pl.BlockSpec((tm, FOX_WIDTH), row), pl.BlockSpec((tm, DSA_WIDTH), row),
                  wspec(w_out.shape), wspec((1, d)), wspec((1, d)),
                  wspec(w_gate.shape), wspec(w_up.shape), wspec(w_down.shape), wspec((1, d)), wspec((1, d))],
        out_specs=pl.BlockSpec((tm, d), row),
        out_shape=jax.ShapeDtypeStruct((n, d), F32),
        compiler_params=pltpu.CompilerParams(dimension_semantics=("parallel",), vmem_limit_bytes=VMEM_LIMIT),
        name="post",
    )(x2, fox_o, dsa_o, w_out.astype(BF16), vec(ln1_g), vec(ln1_b),
      w_gate.astype(BF16), w_up.astype(BF16), w_down.astype(BF16), vec(ln2_g), vec(ln2_b))


def _round_up(n, m):
    return -(-n // m) * m


def _pad_rows(a, rows):
    return jnp.pad(a, ((0, 0), (0, rows - a.shape[1]), (0, 0)))


def _largest_tile(n, candidates):
    return next(c for c in candidates if n % c == 0)


def _mixer_layer(y, past, weights, *, alpha):
    (w_packed, bias, w_out, ln1_g, ln1_b, w_gate, w_up, w_down, ln2_g, ln2_b) = weights
    b, t, d = y.shape
    p_len = 0 if past is None else past[0].shape[1]
    tables = _rope_lane_tables(p_len + jnp.arange(t))
    (fq, fk, fkb, fv, fvb, dq, dk, dv, dk2, dvb, iq, ik, ik2, misc) = _project(
        y.reshape(b * t, d), w_packed, bias, tables, t, min(512, t))
    r3 = lambda a: a.reshape(b, t, a.shape[-1])
    fq, fkb, fvb, dq, dk2, dvb, iq, ik2, misc3 = map(r3, (fq, fkb, fvb, dq, dk2, dvb, iq, ik2, misc))

    t_pad = _round_up(t, LANES)
    neg_c_new = _cumsum(_pad_rows(misc3, t_pad), scale=-LOG2E)[:, :t]
    n_keys = p_len + t
    top_k = min(IDX_TOPK_MAX, n_keys // 4)
    if past is None:
        n_pad = n_keys
        k_all, v_all, kb = fkb, fvb, neg_c_new
        dk_all, dv_all, ik_all = dk2, dvb, ik2
    else:
        cf_k, cf_v, cf_logf, cd_k, cd_v, ci_k = past
        n_pad = _round_up(n_keys, LANES)
        cat = lambda old, new: _pad_rows(jnp.concatenate([old.astype(new.dtype), new], axis=1), n_pad)
        k_all = cat(cf_k.reshape(b, p_len, FOX_WIDTH), fkb)
        v_all = cat(cf_v.reshape(b, p_len, FOX_WIDTH), fvb)
        dk_all = cat(jnp.repeat(cd_k, 2, axis=2).reshape(b, p_len, 2 * LANES), dk2)
        dv_all = cat(cd_v.reshape(b, p_len, LANES), dvb)
        ik_all = cat(jnp.concatenate([ci_k, ci_k], axis=-1), ik2)
        lf_past = jnp.pad(cf_logf.astype(F32), ((0, 0), (0, 0), (0, LANES - FOX_HEADS)))
        d_past = _cumsum(lf_past, reverse=True, inclusive=False, scale=LOG2E)
        kb = cat(d_past, neg_c_new)
    tq = min(512, t_pad)
    tk = _largest_tile(n_pad, (512, 384, 256, 128) if tq >= 512 else (1408, 1024, 512, 384, 256, 128))
    pad_q = lambda a: _pad_rows(a, t_pad)
    w_t = jnp.transpose(misc3[:, :, _MISC_IW:_MISC_IW + SUBLANES], (0, 2, 1))
    w_t = jnp.pad(w_t, ((0, 0), (0, 0), (0, t_pad - t)))
    fox_o = _fox_attn(pad_q(fq), k_all, v_all, kb, q_pos0=p_len, tq=tq, tk=tk)[:, :t]
    dsa_o = _dsa_attn(pad_q(dq), pad_q(iq), w_t, dk_all, dv_all, ik_all, q_pos0=p_len, n_real=n_keys,
                      top_k=top_k, tq=tq, tk=tk)[:, :t]
    y_out = _post(y.reshape(b * t, d), fox_o.reshape(b * t, -1), dsa_o.reshape(b * t, -1),
                  w_out, ln1_g, ln1_b, w_gate, w_up, w_down, ln2_g, ln2_b, alpha=alpha, tm=min(512, b * t))
    states = (fk.reshape(b, t, FOX_HEADS, HEAD_DIM), fv.reshape(b, t, FOX_HEADS, HEAD_DIM),
              misc3[:, :, :FOX_HEADS],
              dk.reshape(b, t, DSA_KV_HEADS, HEAD_DIM), dv.reshape(b, t, DSA_KV_HEADS, HEAD_DIM),
              ik.reshape(b, t, HEAD_DIM))
    return y_out.reshape(b, t, d), states


def kernel(x_prompt, x_sample, cache_fox_k, cache_fox_v, cache_fox_logf, cache_dsa_k, cache_dsa_v, cache_idx_k,
           w_in, b_forget, w_out, ln1_g, ln1_b, w_gate, w_up, w_down, ln2_g, ln2_b):
    depth = w_in.shape[0]
    alpha = (2 * depth) ** 0.25
    yp, ys = x_prompt, x_sample
    p_states, s_states = [], []
    for l in range(depth):
        w_packed, bias = _pack_w_in(w_in[l], b_forget[l])
        weights = (w_packed, bias, w_out[l], ln1_g[l], ln1_b[l], w_gate[l], w_up[l], w_down[l], ln2_g[l], ln2_b[l])
        yp, st = _mixer_layer(yp, None, weights, alpha=alpha)
        p_states.append(st)
        past = (cache_fox_k[l], cache_fox_v[l], cache_fox_logf[l], cache_dsa_k[l], cache_dsa_v[l], cache_idx_k[l])
        ys, st = _mixer_layer(ys, past, weights, alpha=alpha)
        s_states.append(st)
    stack = lambda sts: tuple(jnp.stack([s[n] for s in sts], axis=0) for n in range(6))
    return (yp, ys) + stack(p_states) + stack(s_states)
```

```python
import functools

import numpy as np
import jax
import jax.numpy as jnp
from jax import lax
from jax.experimental import pallas as pl
from jax.experimental.pallas import tpu as pltpu

HEAD_DIM = 64
FOX_HEADS = 8
DSA_HEADS = 8
DSA_KV_HEADS = 2
IDX_HEADS = 4
CHUNK = 64
CHUNK_SHIFT = 6
IDX_TOPK_MAX = 256
ROPE_THETA = 500000.0
ROT_DIM = HEAD_DIM // 4
LN_EPS = 1e-5
LOG2E = 1.4426950408889634

LANES = 128
SUBLANES = 8
FOX_WIDTH = FOX_HEADS * HEAD_DIM
DSA_WIDTH = DSA_HEADS * HEAD_DIM
INT_MIN = -(2 ** 31)
ZERO_BAND = 1 << 13
NEG_INF = float("-inf")
F32 = jnp.float32
BF16 = jnp.bfloat16
VMEM_LIMIT = 56 * 1024 * 1024

_C_FQ, _C_FK, _C_FV, _C_DQ = 0, 512, 1024, 1536
_C_DK, _C_DV, _C_IQ, _C_IK, _C_MISC, _C_END = 2048, 2176, 2304, 2560, 2688, 2816
_MISC_IW = 8


def _nt_dot(a, b):
    return lax.dot_general(a, b, (((1,), (1,)), ((), ())), preferred_element_type=F32)


def _half_mask(shape, hh):
    lane = lax.broadcasted_iota(jnp.int32, shape, len(shape) - 1)
    return (lane < HEAD_DIM) if hh == 0 else (lane >= HEAD_DIM)


def _project_kernel(x_ref, w_ref, bias_ref, c_ref, sa_ref, sb_ref,
                    fq_ref, fk_ref, fkb_ref, fv_ref, fvb_ref, dq_ref, dk_ref, dv_ref,
                    dk2_ref, dvb_ref, iq_ref, ik_ref, ik2_ref, misc_ref):
    xb = x_ref[...].astype(BF16)
    cos, sa, sb = c_ref[...], sa_ref[...], sb_ref[...]

    def mm(lo, hi):
        return jnp.dot(xb, w_ref[:, lo:hi], preferred_element_type=F32)

    def rope(yb):
        return yb * cos + pltpu.roll(yb, LANES - ROT_DIM // 2, 1) * sa + pltpu.roll(yb, ROT_DIM // 2, 1) * sb

    def blocks(y):
        return [y[:, b * LANES:(b + 1) * LANES] for b in range(y.shape[1] // LANES)]

    fq_ref[...] = mm(_C_FQ, _C_FK).astype(BF16)
    y = mm(_C_FK, _C_FV)
    fk_ref[...] = y
    fkb_ref[...] = y.astype(BF16)
    y = mm(_C_FV, _C_DQ)
    fv_ref[...] = y
    fvb_ref[...] = y.astype(BF16)
    for b, yb in enumerate(blocks(mm(_C_DQ, _C_DK))):
        dq_ref[:, b * LANES:(b + 1) * LANES] = rope(yb).astype(BF16)

    half0 = _half_mask((x_ref.shape[0], LANES), 0)
    kv = mm(_C_DK, _C_IQ)
    k = rope(kv[:, :LANES])
    v = kv[:, LANES:]
    for src, dst in ((k, dk_ref), (v, dv_ref)):
        for g in range(DSA_KV_HEADS):
            dst[:, g, :] = src[:, g * HEAD_DIM:(g + 1) * HEAD_DIM]
    dvb_ref[...] = v.astype(BF16)
    sw = pltpu.roll(k, HEAD_DIM, 1)
    dk2_ref[:, :LANES] = jnp.where(half0, k, sw).astype(BF16)
    dk2_ref[:, LANES:] = jnp.where(half0, sw, k).astype(BF16)

    for b, yb in enumerate(blocks(mm(_C_IQ, _C_IK))):
        iq_ref[:, b * LANES:(b + 1) * LANES] = rope(yb).astype(BF16)
    ik = rope(mm(_C_IK, _C_MISC))
    ik_ref[...] = ik[:, :HEAD_DIM]
    ik2_ref[...] = ik.astype(BF16)

    z = mm(_C_MISC, _C_END)
    zf = z + bias_ref[...]
    logf = jnp.minimum(zf, 0.0) - jnp.log1p(jnp.exp(-jnp.abs(zf)))
    lane = lax.broadcasted_iota(jnp.int32, z.shape, 1)
    misc_ref[...] = jnp.where(lane < _MISC_IW, logf, z)


def _pack_w_in(w, b_forget):
    d = w.shape[0]
    sizes = (FOX_WIDTH, FOX_WIDTH, FOX_WIDTH, FOX_HEADS, DSA_WIDTH, DSA_KV_HEADS * HEAD_DIM,
             DSA_KV_HEADS * HEAD_DIM, IDX_HEADS * HEAD_DIM, HEAD_DIM, IDX_HEADS)
    offs = np.concatenate([[0], np.cumsum(sizes)])
    fq, fk, fv, ff, dq, dk, dv, iq, ik, iw = [w[:, offs[i]:offs[i + 1]] for i in range(len(sizes))]
    scale = HEAD_DIM ** -0.5 * LOG2E
    misc = jnp.concatenate([ff, iw, jnp.zeros((d, LANES - FOX_HEADS - IDX_HEADS), w.dtype)], axis=1)
    packed = jnp.concatenate([fq * scale, fk, fv, dq * scale, dk, dv, iq, ik, ik, misc], axis=1)
    bias = jnp.concatenate([b_forget.astype(F32), jnp.zeros((LANES - FOX_HEADS,), F32)])[None, :]
    return packed.astype(BF16), bias


def _rope_lane_tables(positions):
    half = ROT_DIM // 2
    inv_freq = ROPE_THETA ** (-jnp.arange(half, dtype=F32) * 2.0 / ROT_DIM)
    ang = positions.astype(F32)[:, None] * inv_freq[None, :]
    cos, sin = jnp.cos(ang), jnp.sin(ang)
    n = positions.shape[0]
    rest = HEAD_DIM - ROT_DIM
    c = jnp.concatenate([cos, cos, jnp.ones((n, rest), F32)], axis=1)
    sa = jnp.concatenate([-sin, jnp.zeros((n, half + rest), F32)], axis=1)
    sb = jnp.concatenate([jnp.zeros((n, half), F32), sin, jnp.zeros((n, rest), F32)], axis=1)
    rep = LANES // HEAD_DIM
    return tuple(jnp.tile(t, (1, rep)) for t in (c, sa, sb))


def _project(x2, w_packed, bias, tables, seq, tm):
    n, d = x2.shape
    assert seq % tm == 0 and n % seq == 0
    tiles_per_seq = seq // tm
    row = lambda t: (t, 0)
    pos = lambda t: (t % tiles_per_seq, 0)
    const = lambda t: (0, 0)

    def out(width, dtype):
        return jax.ShapeDtypeStruct((n, width), dtype), pl.BlockSpec((tm, width), row)

    kv_state = (jax.ShapeDtypeStruct((n, DSA_KV_HEADS, HEAD_DIM), F32),
                pl.BlockSpec((tm, DSA_KV_HEADS, HEAD_DIM), lambda t: (t, 0, 0)))
    outs = [out(FOX_WIDTH, BF16),
            out(FOX_WIDTH, F32), out(FOX_WIDTH, BF16),
            out(FOX_WIDTH, F32), out(FOX_WIDTH, BF16),
            out(DSA_WIDTH, BF16),
            kv_state, kv_state,
            out(2 * LANES, BF16), out(LANES, BF16),
            out(IDX_HEADS * HEAD_DIM, BF16),
            out(HEAD_DIM, F32), out(LANES, BF16),
            out(LANES, F32)]
    return pl.pallas_call(
        _project_kernel,
        grid=(n // tm,),
        in_specs=[pl.BlockSpec((tm, d), row),
                  pl.BlockSpec((d, _C_END), const, pipeline_mode=pl.Buffered(1)),
                  pl.BlockSpec((1, LANES), const),
                  pl.BlockSpec((tm, LANES), pos), pl.BlockSpec((tm, LANES), pos), pl.BlockSpec((tm, LANES), pos)],
        out_specs=[o[1] for o in outs],
        out_shape=[o[0] for o in outs],
        compiler_params=pltpu.CompilerParams(dimension_semantics=("parallel",), vmem_limit_bytes=VMEM_LIMIT),
        name="project",
    )(x2, w_packed, bias, *tables)


def _cumsum_kernel(x_ref, o_ref, carry_ref, *, reverse, inclusive, scale):
    @pl.when(pl.program_id(1) == 0)
    def _():
        carry_ref[...] = jnp.zeros_like(carry_ref)

    x = x_ref[0]
    tm = x.shape[0]
    hi = x.astype(BF16)
    r1 = x - hi.astype(F32)
    mid = r1.astype(BF16)
    lo = (r1 - mid.astype(F32)).astype(BF16)
    row = lax.broadcasted_iota(jnp.int32, (tm, tm), 0)
    col = lax.broadcasted_iota(jnp.int32, (tm, tm), 1)
    if reverse:
        keep = (col >= row) if inclusive else (col > row)
    else:
        keep = (col <= row) if inclusive else (col < row)
    tri = jnp.where(keep, 1.0, 0.0).astype(BF16)
    cs = (jnp.dot(tri, hi, preferred_element_type=F32) + jnp.dot(tri, mid, preferred_element_type=F32)
          + jnp.dot(tri, lo, preferred_element_type=F32))
    carry = carry_ref[0:1, :]
    o_ref[0] = scale * (cs + carry)
    carry_ref[0:1, :] = carry + jnp.sum(x, axis=0, keepdims=True)


def _cumsum(x, *, reverse=False, inclusive=True, scale=1.0, tm=512):
    b, s, w = x.shape
    tm = min(tm, s)
    assert s % tm == 0 and w == LANES
    nt = s // tm
    idx = (lambda i, t: (i, nt - 1 - t, 0)) if reverse else (lambda i, t: (i, t, 0))
    return pl.pallas_call(
        functools.partial(_cumsum_kernel, reverse=reverse, inclusive=inclusive, scale=scale),
        grid=(b, nt),
        in_specs=[pl.BlockSpec((1, tm, w), idx)],
        out_specs=pl.BlockSpec((1, tm, w), idx),
        out_shape=jax.ShapeDtypeStruct(x.shape, F32),
        scratch_shapes=[pltpu.VMEM((SUBLANES, w), F32)],
        compiler_params=pltpu.CompilerParams(dimension_semantics=("parallel", "arbitrary")),
        name="cumsum",
    )(x)


BIAS_PIECES = 3


def _bias_lane0(h):
    return HEAD_DIM if h % 2 == 0 else 0


def _mask_heads(q_ref, qm_ref, n_heads, bias_ones=False):
    for h in range(n_heads):
        qpair = q_ref[0, :, (h // 2) * LANES:(h // 2 + 1) * LANES]
        other = jnp.zeros_like(qpair)
        if bias_ones:
            lane = lax.broadcasted_iota(jnp.int32, qpair.shape, 1)
            in_bias = jnp.logical_and(lane >= _bias_lane0(h), lane < _bias_lane0(h) + BIAS_PIECES)
            other = jnp.where(in_bias, 1.0, 0.0).astype(qpair.dtype)
        qm_ref[h] = jnp.where(_half_mask(qpair.shape, h % 2), qpair, other)


ONES_ROWS = 16
VAL_ROWS = HEAD_DIM + ONES_ROWS


N_SLOTS = 4


def _flash_scratch(n_heads, tq, tk):
    return [pltpu.VMEM((N_SLOTS, tk, tq), F32), pltpu.VMEM((N_SLOTS, SUBLANES, tq), F32),
            pltpu.VMEM((n_heads, 1, tq), F32), pltpu.VMEM((n_heads, VAL_ROWS, tq), F32)]


def _init_softmax(m_ref, acc_ref):
    m_ref[...] = jnp.full_like(m_ref, NEG_INF)
    acc_ref[...] = jnp.zeros_like(acc_ref)


def _flash_tiles(start, end, logits, values, s_ref, st_ref, m_ref, acc_ref, prepare=None):
    n_heads = m_ref.shape[0]

    def stage_a(j, h):
        slot = h % N_SLOTS
        s = logits(j, h)
        s_ref[slot] = s
        m_prev = m_ref[h]
        m_new = jnp.maximum(m_prev, jnp.max(s, axis=0, keepdims=True))
        m_use = jnp.where(m_new == NEG_INF, 0.0, m_new)
        st_ref[slot, 0:1, :] = m_use
        st_ref[slot, 1:2, :] = jnp.exp2(m_prev - m_use)
        m_ref[h] = m_new

    def stage_b(j, h):
        slot = h % N_SLOTS
        p = jnp.exp2(s_ref[slot] - st_ref[slot, 0:1, :]).astype(BF16)
        alpha = st_ref[slot, 1:2, :]
        acc_ref[h] = alpha * acc_ref[h] + jnp.dot(values(j, h), p, preferred_element_type=F32)

    def later_heads(j):
        for h in range(1, n_heads):
            stage_a(j, h)
            stage_b(j, h - 1)

    def first_head(j):
        if prepare is not None:
            prepare(j)
        stage_a(j, 0)

    def body(j, carry):
        later_heads(j)
        first_head(j + 1)
        stage_b(j, n_heads - 1)
        return carry

    first_head(start)
    lax.fori_loop(start, end - 1, body, 0)
    later_heads(end - 1)
    stage_b(end - 1, n_heads - 1)


def _store_heads(o_ref, acc_ref, n_heads):
    def head(h):
        return acc_ref[h, :HEAD_DIM, :] / acc_ref[h, HEAD_DIM:HEAD_DIM + 1, :]

    for c in range(n_heads // 2):
        ot = jnp.concatenate([head(2 * c), head(2 * c + 1)], axis=0)
        o_ref[0, :, c * LANES:(c + 1) * LANES] = ot.T.astype(o_ref.dtype)


def _key_major(v, tk):
    b, t, c = v.shape
    heads = c // HEAD_DIM
    vt = jnp.transpose(v.reshape(b, t // tk, tk, heads, HEAD_DIM), (0, 1, 3, 4, 2))
    ones = jnp.ones((b, t // tk, heads, ONES_ROWS, tk), v.dtype)
    return jnp.concatenate([vt, ones], axis=3).reshape(b, t // tk, heads * VAL_ROWS, tk)


def _fox_keys_kernel(k_ref, kb_ref, o_ref):
    kb = kb_ref[0]
    hi = kb.astype(BF16).astype(F32)
    r1 = kb - hi
    mid = r1.astype(BF16).astype(F32)
    pieces = (hi, mid, r1 - mid)
    lane = lax.broadcasted_iota(jnp.int32, kb.shape, 1)
    for h in range(FOX_HEADS):
        bias = jnp.zeros_like(kb)
        for n, piece in enumerate(pieces):
            dst = _bias_lane0(h) + n
            bias = jnp.where(lane == dst, pltpu.roll(piece, (dst - h) % LANES, 1), bias)
        kpair = k_ref[0, :, (h // 2) * LANES:(h // 2 + 1) * LANES]
        o_ref[0, :, h * LANES:(h + 1) * LANES] = jnp.where(_half_mask(kpair.shape, h % 2), kpair, bias.astype(BF16))


def _fox_keys(k, kb, tm=512):
    b, t, _ = k.shape
    tm = _largest_tile(t, (tm, 384, 256, 128))
    spec = lambda w: pl.BlockSpec((1, tm, w), lambda bb, i: (bb, i, 0))
    return pl.pallas_call(
        _fox_keys_kernel,
        grid=(b, t // tm),
        in_specs=[spec(FOX_WIDTH), spec(LANES)],
        out_specs=spec(FOX_HEADS * LANES),
        out_shape=jax.ShapeDtypeStruct((b, t, FOX_HEADS * LANES), BF16),
        compiler_params=pltpu.CompilerParams(dimension_semantics=("parallel", "parallel")),
        name="fox_keys",
    )(k, kb)


def _fox_kernel(q_ref, k_ref, vt_ref, o_ref, qm_ref, s_ref, st_ref, m_ref, acc_ref, *, tq, tk, q_pos0, nk):
    row0 = q_pos0 + pl.program_id(1) * tq
    n_full = jnp.minimum(nk, (row0 + 1) // tk)
    j_end = jnp.minimum(nk, (row0 + tq - 1) // tk + 1)
    _mask_heads(q_ref, qm_ref, FOX_HEADS, bias_ones=True)
    _init_softmax(m_ref, acc_ref)

    def logits(j, h, masked):
        off = pl.multiple_of(j * tk, tk)
        s = _nt_dot(k_ref[0, pl.ds(off, tk), h * LANES:(h + 1) * LANES], qm_ref[h])
        if masked:
            key_pos = j * tk + lax.broadcasted_iota(jnp.int32, (tk, tq), 0)
            s = jnp.where(key_pos <= row0 + lax.broadcasted_iota(jnp.int32, (tk, tq), 1), s, NEG_INF)
        return s

    def values(j, h):
        return vt_ref[0, j, h * VAL_ROWS:(h + 1) * VAL_ROWS, :]

    def run(start, end, masked):
        @pl.when(end > start)
        def _():
            _flash_tiles(start, end, functools.partial(logits, masked=masked), values, s_ref, st_ref, m_ref, acc_ref)

    run(0, n_full, False)
    run(n_full, j_end, True)
    _store_heads(o_ref, acc_ref, FOX_HEADS)


def _fox_attn(q, k, v, kb, *, q_pos0, tq, tk):
    b, tq_all, _ = q.shape
    tk_all = k.shape[1]
    assert tq_all % tq == 0 and tk_all % tk == 0 and tq % LANES == 0
    nk = tk_all // tk
    vt = _key_major(v, tk)
    k_aug = _fox_keys(k, kb)
    return pl.pallas_call(
        functools.partial(_fox_kernel, tq=tq, tk=tk, q_pos0=q_pos0, nk=nk),
        grid=(b, tq_all // tq),
        in_specs=[pl.BlockSpec((1, tq, FOX_WIDTH), lambda bb, i: (bb, i, 0)),
                  pl.BlockSpec((1, tk_all, FOX_HEADS * LANES), lambda bb, i: (bb, 0, 0)),
                  pl.BlockSpec((1, nk, FOX_HEADS * VAL_ROWS, tk), lambda bb, i: (bb, 0, 0, 0))],
        out_specs=pl.BlockSpec((1, tq, FOX_WIDTH), lambda bb, i: (bb, i, 0)),
        out_shape=jax.ShapeDtypeStruct(q.shape, BF16),
        scratch_shapes=[pltpu.VMEM((FOX_HEADS, tq, LANES), BF16)] + _flash_scratch(FOX_HEADS, tq, tk),
        compiler_params=pltpu.CompilerParams(dimension_semantics=("parallel", "arbitrary"),
                                             vmem_limit_bytes=VMEM_LIMIT),
        name="fox_attn",
    )(q, k_aug, vt)


def _dsa_kernel(dq_ref, iq_ref, w_ref, dk_ref, dvt_ref, ik_ref, o_ref,
                qm_ref, iqm_ref, key_ref, hi_ref, lo_ref, drop_ref, s_ref, st_ref, m_ref, acc_ref,
                *, tq, tk, q_pos0, n_real, top_k, nk):
    row0 = q_pos0 + pl.program_id(1) * tq
    q_pos = row0 + lax.broadcasted_iota(jnp.int32, (1, tq), 1)
    lim = jnp.minimum(lax.shift_left(jnp.right_shift(q_pos, CHUNK_SHIFT) + 1, CHUNK_SHIFT), n_real)
    lim_first = jnp.minimum((row0 // CHUNK + 1) * CHUNK, n_real)
    lim_last = jnp.minimum(((row0 + tq - 1) // CHUNK + 1) * CHUNK, n_real)
    n_full = lim_first // tk
    nt = (lim_last + tk - 1) // tk
    _mask_heads(dq_ref, qm_ref, DSA_HEADS)
    _mask_heads(iq_ref, iqm_ref, IDX_HEADS)

    def ktile(ref, j, lo, hi):
        return ref[0, pl.ds(pl.multiple_of(j * tk, tk), tk), lo:hi]

    def key_pos(j):
        return j * tk + lax.broadcasted_iota(jnp.int32, (tk, tq), 0)

    def score_tile(j, edge):
        ikt = ktile(ik_ref, j, 0, LANES)
        sc = jnp.zeros((tk, tq), F32)
        for h in range(IDX_HEADS):
            sc = sc + w_ref[0, h:h + 1, :] * jnp.maximum(_nt_dot(ikt, iqm_ref[h]), 0.0)
        bits = pltpu.bitcast(sc, jnp.int32)
        key = jnp.where(bits < 0, (bits ^ 0x7FFFFFFF) + 1 - ZERO_BAND, bits)
        pos = key_pos(j)
        key = jnp.where(sc == 0.0, -1 - pos, key)
        if edge:
            key = jnp.where(pos < lim, key, INT_MIN)
        key_ref[j] = key
        hi_ref[j] = jnp.right_shift(key, 16).astype(jnp.int16)
        lo_ref[j] = (jnp.bitwise_and(key, 0xFFFF) - 32768).astype(jnp.int16)

    def score_full(j, carry):
        score_tile(j, False)
        return carry

    def score_edge(j, carry):
        score_tile(j, True)
        return carry

    lax.fori_loop(0, n_full, score_full, 0)
    lax.fori_loop(n_full, nt, score_edge, 0)

    def count16(ref, cand):
        cand16 = cand.astype(jnp.int16)

        def body(j, acc):
            hit = jnp.where(ref[j] >= cand16, jnp.asarray(1, BF16), jnp.asarray(0, BF16))
            part = hit[0:16, :]
            for g in range(1, tk // 16):
                part = part + hit[g * 16:(g + 1) * 16, :]
            return acc + part.astype(F32)

        acc = lax.fori_loop(0, nt, body, jnp.zeros((16, tq), F32))
        return jnp.sum(acc, axis=0, keepdims=True)

    def bisect16(ref, base, cnt0):
        def step(it, carry):
            t, cnt_t = carry
            cand = t + lax.shift_left(jnp.int32(1), 15 - it)
            cnt = base + count16(ref, cand)
            ok = cnt >= top_k
            return jnp.where(ok, cand, t), jnp.where(ok, cnt, cnt_t)
        return lax.fori_loop(0, 16, step, (jnp.full((1, tq), -32768, jnp.int32), cnt0))

    t_hi, cnt_hi = bisect16(hi_ref, 0.0, jnp.full((1, tq), 2.0 ** 30, F32))
    above = jnp.where(t_hi >= 32767, 0.0, count16(hi_ref, jnp.minimum(t_hi + 1, 32767)))
    t_hi16 = t_hi.astype(jnp.int16)

    def keep_low(j, carry):
        lo_ref[j] = jnp.where(hi_ref[j] == t_hi16, lo_ref[j], jnp.asarray(-32768, jnp.int16))
        return carry

    lax.fori_loop(0, nt, keep_low, 0)
    t_lo, cnt_thr = bisect16(lo_ref, above, cnt_hi)
    thr = t_hi * 65536 + (t_lo + 32768)

    def count(pred):
        def body(j, acc):
            hit = jnp.where(pred(key_ref[j], j), 1.0, 0.0)
            return acc + jnp.sum(hit.reshape(tk // SUBLANES, SUBLANES, tq), axis=0)
        acc = lax.fori_loop(0, nt, body, jnp.zeros((SUBLANES, tq), F32))
        return jnp.sum(acc, axis=0, keepdims=True)

    excess = jnp.logical_and(cnt_thr > top_k, thr > INT_MIN)

    @pl.when(jnp.max(jnp.where(excess, 1.0, 0.0)) > 0.0)
    def _():
        need = top_k - count(lambda kt, j: kt > thr)
        bound = jnp.zeros((1, tq), jnp.int32)
        for bit in reversed(range(int(nk * tk - 1).bit_length())):
            cand = bound + (1 << bit)
            c = count(lambda kt, j: jnp.logical_and(kt == thr, key_pos(j) < cand))
            bound = jnp.where(c < need, cand, bound)

        def demote(j, carry):
            kt = key_ref[j]
            drop = jnp.logical_and(jnp.logical_and(kt == thr, key_pos(j) > bound), excess)
            key_ref[j] = jnp.where(drop, kt - 1, kt)
            return carry

        lax.fori_loop(0, nt, demote, 0)

    thr_sel = jnp.maximum(thr, INT_MIN + 1)
    _init_softmax(m_ref, acc_ref)
    kv_head = lambda h: h // (DSA_HEADS // DSA_KV_HEADS)

    def select(j):
        drop_ref[...] = jnp.where(key_ref[j] >= thr_sel, 0.0, NEG_INF)

    def logits(j, h):
        g = kv_head(h)
        return _nt_dot(ktile(dk_ref, j, g * LANES, (g + 1) * LANES), qm_ref[h]) + drop_ref[...]

    def values(j, h):
        return dvt_ref[0, j, kv_head(h) * VAL_ROWS:(kv_head(h) + 1) * VAL_ROWS, :]

    _flash_tiles(0, nt, logits, values, s_ref, st_ref, m_ref, acc_ref, prepare=select)
    _store_heads(o_ref, acc_ref, DSA_HEADS)


def _dsa_attn(dq, iq, w_t, dk2, dv, ik2, *, q_pos0, n_real, top_k, tq, tk):
    b, tq_all, _ = dq.shape
    tk_all = dk2.shape[1]
    assert tq_all % tq == 0 and tk_all % tk == 0 and tq % LANES == 0 and tk % 16 == 0 and tk_all <= ZERO_BAND
    nk = tk_all // tk
    dvt = _key_major(dv, tk)
    qspec = lambda w: pl.BlockSpec((1, tq, w), lambda bb, i: (bb, i, 0))
    kspec = lambda w: pl.BlockSpec((1, tk_all, w), lambda bb, i: (bb, 0, 0))
    return pl.pallas_call(
        functools.partial(_dsa_kernel, tq=tq, tk=tk, q_pos0=q_pos0, n_real=n_real, top_k=top_k, nk=nk),
        grid=(b, tq_all // tq),
        in_specs=[qspec(DSA_WIDTH), qspec(IDX_HEADS * HEAD_DIM),
                  pl.BlockSpec((1, SUBLANES, tq), lambda bb, i: (bb, 0, i)),
                  kspec(2 * LANES),
                  pl.BlockSpec((1, nk, DSA_KV_HEADS * VAL_ROWS, tk), lambda bb, i: (bb, 0, 0, 0)),
                  kspec(LANES)],
        out_specs=qspec(DSA_WIDTH),
        out_shape=jax.ShapeDtypeStruct(dq.shape, BF16),
        scratch_shapes=[pltpu.VMEM((DSA_HEADS, tq, LANES), BF16), pltpu.VMEM((IDX_HEADS, tq, LANES), BF16),
                        pltpu.VMEM((nk, tk, tq), jnp.int32), pltpu.VMEM((nk, tk, tq), jnp.int16),
                        pltpu.VMEM((nk, tk, tq), jnp.int16), pltpu.VMEM((tk, tq), F32)]
                       + _flash_scratch(DSA_HEADS, tq, tk),
        compiler_params=pltpu.CompilerParams(dimension_semantics=("parallel", "arbitrary"),
                                             vmem_limit_bytes=VMEM_LIMIT),
        name="dsa_attn",
    )(dq, iq, w_t, dk2, dvt, ik2)


def _layer_norm(x, g, b):
    mu = jnp.mean(x, axis=-1, keepdims=True)
    xc = x - mu
    var = jnp.mean(xc * xc, axis=-1, keepdims=True)
    return xc * lax.rsqrt(var + LN_EPS) * g + b


def _post_kernel(x_ref, fox_ref, dsa_ref, wo_ref, g1_ref, b1_ref, wg_ref, wu_ref, wd_ref, g2_ref, b2_ref,
                 o_ref, *, alpha, ff_chunk):
    mix = (jnp.dot(fox_ref[...], wo_ref[:FOX_WIDTH, :], preferred_element_type=F32)
           + jnp.dot(dsa_ref[...], wo_ref[FOX_WIDTH:, :], preferred_element_type=F32))
    h = _layer_norm(alpha * x_ref[...] + mix, g1_ref[...], b1_ref[...])
    hb = h.astype(BF16)
    f = jnp.zeros_like(h)
    for c in range(wg_ref.shape[1] // ff_chunk):
        lo, hi = c * ff_chunk, (c + 1) * ff_chunk
        gate = jnp.dot(hb, wg_ref[:, lo:hi], preferred_element_type=F32)
        up = jnp.dot(hb, wu_ref[:, lo:hi], preferred_element_type=F32)
        act = (gate * jax.nn.sigmoid(gate) * up).astype(BF16)
        f = f + jnp.dot(act, wd_ref[lo:hi, :], preferred_element_type=F32)
    o_ref[...] = _layer_norm(alpha * h + f, g2_ref[...], b2_ref[...])


def _post(x2, fox_o, dsa_o, w_out, ln1_g, ln1_b, w_gate, w_up, w_down, ln2_g, ln2_b, *, alpha, tm, ff_chunk=256):
    n, d = x2.shape
    d_ff = w_gate.shape[1]
    assert n % tm == 0 and d_ff % ff_chunk == 0
    row = lambda t: (t, 0)
    const = lambda t: (0, 0)
    wspec = lambda shape: pl.BlockSpec(shape, const, pipeline_mode=pl.Buffered(1))
    vec = lambda a: a.astype(F32)[None, :]
    return pl.pallas_call(
        functools.partial(_post_kernel, alpha=alpha, ff_chunk=ff_chunk),
        grid=(n // tm,),
        in_specs=[pl.BlockSpec((tm, d), row), pl.BlockSpec((tm, FOX_WIDTH), row), pl.BlockSpec((tm, DSA_WIDTH), row),
                  wspec(w_out.shape), wspec((1, d)), wspec((1, d)),
                  wspec(w_gate.shape), wspec(w_up.shape), wspec(w_down.shape), wspec((1, d)), wspec((1, d))],
        out_specs=pl.BlockSpec((tm, d), row),
        out_shape=jax.ShapeDtypeStruct((n, d), F32),
        compiler_params=pltpu.CompilerParams(dimension_semantics=("parallel",), vmem_limit_bytes=VMEM_LIMIT),
        name="post",
    )(x2, fox_o, dsa_o, w_out.astype(BF16), vec(ln1_g), vec(ln1_b),
      w_gate.astype(BF16), w_up.astype(BF16), w_down.astype(BF16), vec(ln2_g), vec(ln2_b))


def _round_up(n, m):
    return -(-n // m) * m


def _pad_rows(a, rows):
    return jnp.pad(a, ((0, 0), (0, rows - a.shape[1]), (0, 0)))


def _largest_tile(n, candidates):
    return next(c for c in candidates if n % c == 0)


def _mixer_layer(y, past, weights, *, alpha):
    (w_packed, bias, w_out, ln1_g, ln1_b, w_gate, w_up, w_down, ln2_g, ln2_b) = weights
    b, t, d = y.shape
    p_len = 0 if past is None else past[0].shape[1]
    tables = _rope_lane_tables(p_len + jnp.arange(t))
    (fq, fk, fkb, fv, fvb, dq, dk, dv, dk2, dvb, iq, ik, ik2, misc) = _project(
        y.reshape(b * t, d), w_packed, bias, tables, t, min(512, t))
    r3 = lambda a: a.reshape(b, t, a.shape[-1])
    fq, fkb, fvb, dq, dk2, dvb, iq, ik2, misc3 = map(r3, (fq, fkb, fvb, dq, dk2, dvb, iq, ik2, misc))

    t_pad = _round_up(t, LANES)
    neg_c_new = _cumsum(_pad_rows(misc3, t_pad), scale=-LOG2E)[:, :t]
    n_keys = p_len + t
    top_k = min(IDX_TOPK_MAX, n_keys // 4)
    if past is None:
        n_pad = n_keys
        k_all, v_all, kb = fkb, fvb, neg_c_new
        dk_all, dv_all, ik_all = dk2, dvb, ik2
    else:
        cf_k, cf_v, cf_logf, cd_k, cd_v, ci_k = past
        n_pad = _round_up(n_keys, LANES)
        cat = lambda old, new: _pad_rows(jnp.concatenate([old.astype(new.dtype), new], axis=1), n_pad)
        k_all = cat(cf_k.reshape(b, p_len, FOX_WIDTH), fkb)
        v_all = cat(cf_v.reshape(b, p_len, FOX_WIDTH), fvb)
        dk_all = cat(jnp.repeat(cd_k, 2, axis=2).reshape(b, p_len, 2 * LANES), dk2)
        dv_all = cat(cd_v.reshape(b, p_len, LANES), dvb)
        ik_all = cat(jnp.concatenate([ci_k, ci_k], axis=-1), ik2)
        lf_past = jnp.pad(cf_logf.astype(F32), ((0, 0), (0, 0), (0, LANES - FOX_HEADS)))
        d_past = _cumsum(lf_past, reverse=True, inclusive=False, scale=LOG2E)
        kb = cat(d_past, neg_c_new)
    tq = min(512, t_pad)
    tk = _largest_tile(n_pad, (512, 384, 256, 128) if tq >= 512 else (1408, 1024, 512, 384, 256, 128))
    pad_q = lambda a: _pad_rows(a, t_pad)
    w_t = jnp.transpose(misc3[:, :, _MISC_IW:_MISC_IW + SUBLANES], (0, 2, 1))
    w_t = jnp.pad(w_t, ((0, 0), (0, 0), (0, t_pad - t)))
    fox_o = _fox_attn(pad_q(fq), k_all, v_all, kb, q_pos0=p_len, tq=tq, tk=tk)[:, :t]
    dsa_o = _dsa_attn(pad_q(dq), pad_q(iq), w_t, dk_all, dv_all, ik_all, q_pos0=p_len, n_real=n_keys,
                      top_k=top_k, tq=tq, tk=tk)[:, :t]
    y_out = _post(y.reshape(b * t, d), fox_o.reshape(b * t, -1), dsa_o.reshape(b * t, -1),
                  w_out, ln1_g, ln1_b, w_gate, w_up, w_down, ln2_g, ln2_b, alpha=alpha, tm=min(512, b * t))
    states = (fk.reshape(b, t, FOX_HEADS, HEAD_DIM), fv.reshape(b, t, FOX_HEADS, HEAD_DIM),
              misc3[:, :, :FOX_HEADS],
              dk.reshape(b, t, DSA_KV_HEADS, HEAD_DIM), dv.reshape(b, t, DSA_KV_HEADS, HEAD_DIM),
              ik.reshape(b, t, HEAD_DIM))
    return y_out.reshape(b, t, d), states


def kernel(x_prompt, x_sample, cache_fox_k, cache_fox_v, cache_fox_logf, cache_dsa_k, cache_dsa_v, cache_idx_k,
           w_in, b_forget, w_out, ln1_g, ln1_b, w_gate, w_up, w_down, ln2_g, ln2_b):
    depth = w_in.shape[0]
    alpha = (2 * depth) ** 0.25
    yp, ys = x_prompt, x_sample
    p_states, s_states = [], []
    for l in range(depth):
        w_packed, bias = _pack_w_in(w_in[l], b_forget[l])
        weights = (w_packed, bias, w_out[l], ln1_g[l], ln1_b[l], w_gate[l], w_up[l], w_down[l], ln2_g[l], ln2_b[l])
        yp, st = _mixer_layer(yp, None, weights, alpha=alpha)
        p_states.append(st)
        past = (cache_fox_k[l], cache_fox_v[l], cache_fox_logf[l], cache_dsa_k[l], cache_dsa_v[l], cache_idx_k[l])
        ys, st = _mixer_layer(ys, past, weights, alpha=alpha)
        s_states.append(st)
    stack = lambda sts: tuple(jnp.stack([s[n] for s in sts], axis=0) for n in range(6))
    return (yp, ys) + stack(p_states) + stack(s_states)
```

```python
import functools

import numpy as np
import jax
import jax.numpy as jnp
from jax import lax
from jax.experimental import pallas as pl
from jax.experimental.pallas import tpu as pltpu

HEAD_DIM = 64
FOX_HEADS = 8
DSA_HEADS = 8
DSA_KV_HEADS = 2
IDX_HEADS = 4
CHUNK = 64
CHUNK_SHIFT = 6
IDX_TOPK_MAX = 256
ROPE_THETA = 500000.0
ROT_DIM = HEAD_DIM // 4
LN_EPS = 1e-5
LOG2E = 1.4426950408889634

LANES = 128
SUBLANES = 8
FOX_WIDTH = FOX_HEADS * HEAD_DIM
DSA_WIDTH = DSA_HEADS * HEAD_DIM
INT_MIN = -(2 ** 31)
ZERO_BAND = 1 << 13
NEG_INF = float("-inf")
F32 = jnp.float32
BF16 = jnp.bfloat16
VMEM_LIMIT = 56 * 1024 * 1024

_C_FQ, _C_FK, _C_FV, _C_DQ = 0, 512, 1024, 1536
_C_DK, _C_DV, _C_IQ, _C_IK, _C_MISC, _C_END = 2048, 2176, 2304, 2560, 2688, 2816
_MISC_IW = 8


def _nt_dot(a, b):
    return lax.dot_general(a, b, (((1,), (1,)), ((), ())), preferred_element_type=F32)


def _half_mask(shape, hh):
    lane = lax.broadcasted_iota(jnp.int32, shape, len(shape) - 1)
    return (lane < HEAD_DIM) if hh == 0 else (lane >= HEAD_DIM)


def _project_kernel(x_ref, w_ref, bias_ref, c_ref, sa_ref, sb_ref,
                    fq_ref, fk_ref, fkb_ref, fv_ref, fvb_ref, dq_ref, dk_ref, dv_ref,
                    dk2_ref, dvb_ref, iq_ref, ik_ref, ik2_ref, misc_ref, logf_ref):
    xb = x_ref[...].astype(BF16)
    cos, sa, sb = c_ref[...], sa_ref[...], sb_ref[...]

    def mm(lo, hi):
        return jnp.dot(xb, w_ref[:, lo:hi], preferred_element_type=F32)

    def rope(yb):
        return yb * cos + pltpu.roll(yb, LANES - ROT_DIM // 2, 1) * sa + pltpu.roll(yb, ROT_DIM // 2, 1) * sb

    def blocks(y):
        return [y[:, b * LANES:(b + 1) * LANES] for b in range(y.shape[1] // LANES)]

    fq_ref[...] = mm(_C_FQ, _C_FK).astype(BF16)
    y = mm(_C_FK, _C_FV)
    fk_ref[...] = y
    fkb_ref[...] = y.astype(BF16)
    y = mm(_C_FV, _C_DQ)
    fv_ref[...] = y
    fvb_ref[...] = y.astype(BF16)
    for b, yb in enumerate(blocks(mm(_C_DQ, _C_DK))):
        dq_ref[:, b * LANES:(b + 1) * LANES] = rope(yb).astype(BF16)

    half0 = _half_mask((x_ref.shape[0], LANES), 0)
    kv = mm(_C_DK, _C_IQ)
    k = rope(kv[:, :LANES])
    v = kv[:, LANES:]
    for src, dst in ((k, dk_ref), (v, dv_ref)):
        for g in range(DSA_KV_HEADS):
            dst[:, g, :] = src[:, g * HEAD_DIM:(g + 1) * HEAD_DIM]
    dvb_ref[...] = v.astype(BF16)
    sw = pltpu.roll(k, HEAD_DIM, 1)
    dk2_ref[:, :LANES] = jnp.where(half0, k, sw).astype(BF16)
    dk2_ref[:, LANES:] = jnp.where(half0, sw, k).astype(BF16)

    for b, yb in enumerate(blocks(mm(_C_IQ, _C_IK))):
        iq_ref[:, b * LANES:(b + 1) * LANES] = rope(yb).astype(BF16)
    ik = rope(mm(_C_IK, _C_MISC))
    ik_ref[...] = ik[:, :HEAD_DIM]
    ik2_ref[...] = ik.astype(BF16)

    z = mm(_C_MISC, _C_END)
    zf = z + bias_ref[...]
    logf = jnp.minimum(zf, 0.0) - jnp.log1p(jnp.exp(-jnp.abs(zf)))
    lane = lax.broadcasted_iota(jnp.int32, z.shape, 1)
    misc_ref[...] = jnp.where(lane < _MISC_IW, logf, z)
    logf_ref[...] = logf[:, :FOX_HEADS]


def _pack_w_in(w, b_forget):
    d = w.shape[0]
    sizes = (FOX_WIDTH, FOX_WIDTH, FOX_WIDTH, FOX_HEADS, DSA_WIDTH, DSA_KV_HEADS * HEAD_DIM,
             DSA_KV_HEADS * HEAD_DIM, IDX_HEADS * HEAD_DIM, HEAD_DIM, IDX_HEADS)
    offs = np.concatenate([[0], np.cumsum(sizes)])
    fq, fk, fv, ff, dq, dk, dv, iq, ik, iw = [w[:, offs[i]:offs[i + 1]] for i in range(len(sizes))]
    scale = HEAD_DIM ** -0.5 * LOG2E
    misc = jnp.concatenate([ff, iw, jnp.zeros((d, LANES - FOX_HEADS - IDX_HEADS), w.dtype)], axis=1)
    packed = jnp.concatenate([fq * scale, fk, fv, dq * scale, dk, dv, iq, ik, ik, misc], axis=1)
    bias = jnp.concatenate([b_forget.astype(F32), jnp.zeros((LANES - FOX_HEADS,), F32)])[None, :]
    return packed.astype(BF16), bias


def _rope_lane_tables(positions):
    half = ROT_DIM // 2
    inv_freq = ROPE_THETA ** (-jnp.arange(half, dtype=F32) * 2.0 / ROT_DIM)
    ang = positions.astype(F32)[:, None] * inv_freq[None, :]
    cos, sin = jnp.cos(ang), jnp.sin(ang)
    n = positions.shape[0]
    rest = HEAD_DIM - ROT_DIM
    c = jnp.concatenate([cos, cos, jnp.ones((n, rest), F32)], axis=1)
    sa = jnp.concatenate([-sin, jnp.zeros((n, half + rest), F32)], axis=1)
    sb = jnp.concatenate([jnp.zeros((n, half), F32), sin, jnp.zeros((n, rest), F32)], axis=1)
    rep = LANES // HEAD_DIM
    return tuple(jnp.tile(t, (1, rep)) for t in (c, sa, sb))


def _project(x2, w_packed, bias, tables, seq, tm):
    n, d = x2.shape
    assert seq % tm == 0 and n % seq == 0
    tiles_per_seq = seq // tm
    row = lambda t: (t, 0)
    pos = lambda t: (t % tiles_per_seq, 0)
    const = lambda t: (0, 0)

    def out(width, dtype):
        return jax.ShapeDtypeStruct((n, width), dtype), pl.BlockSpec((tm, width), row)

    kv_state = (jax.ShapeDtypeStruct((n, DSA_KV_HEADS, HEAD_DIM), F32),
                pl.BlockSpec((tm, DSA_KV_HEADS, HEAD_DIM), lambda t: (t, 0, 0)))
    outs = [out(FOX_WIDTH, BF16),
            out(FOX_WIDTH, F32), out(FOX_WIDTH, BF16),
            out(FOX_WIDTH, F32), out(FOX_WIDTH, BF16),
            out(DSA_WIDTH, BF16),
            kv_state, kv_state,
            out(2 * LANES, BF16), out(LANES, BF16),
            out(IDX_HEADS * HEAD_DIM, BF16),
            out(HEAD_DIM, F32), out(LANES, BF16),
            out(LANES, F32),
            out(FOX_HEADS, F32)]
    return pl.pallas_call(
        _project_kernel,
        grid=(n // tm,),
        in_specs=[pl.BlockSpec((tm, d), row),
                  pl.BlockSpec((d, _C_END), const, pipeline_mode=pl.Buffered(1)),
                  pl.BlockSpec((1, LANES), const),
                  pl.BlockSpec((tm, LANES), pos), pl.BlockSpec((tm, LANES), pos), pl.BlockSpec((tm, LANES), pos)],
        out_specs=[o[1] for o in outs],
        out_shape=[o[0] for o in outs],
        compiler_params=pltpu.CompilerParams(dimension_semantics=("parallel",), vmem_limit_bytes=VMEM_LIMIT),
        name="project",
    )(x2, w_packed, bias, *tables)


def _cumsum_kernel(x_ref, o_ref, carry_ref, *, reverse, inclusive, scale):
    @pl.when(pl.program_id(1) == 0)
    def _():
        carry_ref[...] = jnp.zeros_like(carry_ref)

    x = x_ref[0]
    tm = x.shape[0]
    hi = x.astype(BF16)
    r1 = x - hi.astype(F32)
    mid = r1.astype(BF16)
    lo = (r1 - mid.astype(F32)).astype(BF16)
    row = lax.broadcasted_iota(jnp.int32, (tm, tm), 0)
    col = lax.broadcasted_iota(jnp.int32, (tm, tm), 1)
    if reverse:
        keep = (col >= row) if inclusive else (col > row)
    else:
        keep = (col <= row) if inclusive else (col < row)
    tri = jnp.where(keep, 1.0, 0.0).astype(BF16)
    cs = (jnp.dot(tri, hi, preferred_element_type=F32) + jnp.dot(tri, mid, preferred_element_type=F32)
          + jnp.dot(tri, lo, preferred_element_type=F32))
    carry = carry_ref[0:1, :]
    o_ref[0] = scale * (cs + carry)
    carry_ref[0:1, :] = carry + jnp.sum(x, axis=0, keepdims=True)


def _cumsum(x, *, reverse=False, inclusive=True, scale=1.0, tm=512):
    b, s, w = x.shape
    tm = min(tm, s)
    assert s % tm == 0 and w == LANES
    nt = s // tm
    idx = (lambda i, t: (i, nt - 1 - t, 0)) if reverse else (lambda i, t: (i, t, 0))
    return pl.pallas_call(
        functools.partial(_cumsum_kernel, reverse=reverse, inclusive=inclusive, scale=scale),
        grid=(b, nt),
        in_specs=[pl.BlockSpec((1, tm, w), idx)],
        out_specs=pl.BlockSpec((1, tm, w), idx),
        out_shape=jax.ShapeDtypeStruct(x.shape, F32),
        scratch_shapes=[pltpu.VMEM((SUBLANES, w), F32)],
        compiler_params=pltpu.CompilerParams(dimension_semantics=("parallel", "arbitrary")),
        name="cumsum",
    )(x)


BIAS_PIECES = 3


def _bias_lanes(h):
    base = HEAD_DIM if h % 2 == 0 else 0
    return tuple(base + n * FOX_HEADS for n in range(BIAS_PIECES))


def _is_bias_lane(lane, h):
    hit = lane == _bias_lanes(h)[0]
    for dst in _bias_lanes(h)[1:]:
        hit = jnp.logical_or(hit, lane == dst)
    return hit


def _mask_heads(q_ref, qm_ref, n_heads, bias_ones=False):
    for h in range(n_heads):
        qpair = q_ref[0, :, (h // 2) * LANES:(h // 2 + 1) * LANES]
        other = jnp.zeros_like(qpair)
        if bias_ones:
            lane = lax.broadcasted_iota(jnp.int32, qpair.shape, 1)
            other = jnp.where(_is_bias_lane(lane, h), 1.0, 0.0).astype(qpair.dtype)
        qm_ref[h] = jnp.where(_half_mask(qpair.shape, h % 2), qpair, other)


ONES_ROWS = 16
VAL_ROWS = HEAD_DIM + ONES_ROWS


def _flash_scratch(n_heads, tq, tk):
    return [pltpu.VMEM((2, n_heads, tk, tq), F32), pltpu.VMEM((2, n_heads, SUBLANES, tq), F32),
            pltpu.VMEM((n_heads, 1, tq), F32), pltpu.VMEM((n_heads, VAL_ROWS, tq), F32)]


def _init_softmax(m_ref, acc_ref):
    m_ref[...] = jnp.full_like(m_ref, NEG_INF)
    acc_ref[...] = jnp.zeros_like(acc_ref)


def _flash_loop(segments, logits0, values, s_ref, st_ref, m_ref, acc_ref):
    def by_parity(j, fn):
        for slot in range(2):
            pl.when(lax.rem(j, 2) == slot)(functools.partial(fn, slot))

    n_heads = m_ref.shape[0]

    def stage_a(h, s, slot):
        s_ref[slot, h] = s
        m_prev = m_ref[h]
        m_new = jnp.maximum(m_prev, jnp.max(s, axis=0, keepdims=True))
        m_use = jnp.where(m_new == NEG_INF, 0.0, m_new)
        st_ref[slot, h, 0:1, :] = m_use
        st_ref[slot, h, 1:2, :] = jnp.exp2(m_prev - m_use)
        m_ref[h] = m_new

    def stage_b(j, h, slot):
        p = jnp.exp2(s_ref[slot, h] - st_ref[slot, h, 0:1, :]).astype(BF16)
        alpha = st_ref[slot, h, 1:2, :]
        acc_ref[h] = alpha * acc_ref[h] + jnp.dot(values(j, h), p, preferred_element_type=F32)

    for h, s in enumerate(logits0(0)):
        stage_a(h, s, 0)
    start = 1
    for end, logits in segments:
        def both(slot, j, logits):
            tiles = logits(j)
            for h in range(n_heads):
                stage_b(j - 1, h, 1 - slot)
                stage_a(h, next(tiles), slot)

        def body(j, carry, logits=logits):
            by_parity(j, functools.partial(both, j=j, logits=logits))
            return carry

        lax.fori_loop(start, end, body, 0)
        start = jnp.maximum(start, end)
    def drain(slot):
        for h in range(n_heads):
            stage_b(start - 1, h, slot)

    by_parity(start - 1, drain)


def _store_heads(o_ref, acc_ref, n_heads):
    def head(h):
        return acc_ref[h, :HEAD_DIM, :] / acc_ref[h, HEAD_DIM:HEAD_DIM + 1, :]

    for c in range(n_heads // 2):
        ot = jnp.concatenate([head(2 * c), head(2 * c + 1)], axis=0)
        o_ref[0, :, c * LANES:(c + 1) * LANES] = ot.T.astype(o_ref.dtype)


def _key_major(v, tk):
    b, t, c = v.shape
    heads = c // HEAD_DIM
    vt = jnp.transpose(v.reshape(b, t // tk, tk, heads, HEAD_DIM), (0, 1, 3, 4, 2))
    ones = jnp.ones((b, t // tk, heads, ONES_ROWS, tk), v.dtype)
    return jnp.concatenate([vt, ones], axis=3).reshape(b, t // tk, heads * VAL_ROWS, tk)


def _fox_keys_kernel(k_ref, kb_ref, o_ref):
    kb = kb_ref[0]
    hi = kb.astype(BF16).astype(F32)
    r1 = kb - hi
    mid = r1.astype(BF16).astype(F32)
    lane = lax.broadcasted_iota(jnp.int32, kb.shape, 1)
    packed = jnp.zeros_like(kb)
    for n, piece in enumerate((hi, mid, r1 - mid)):
        in_group = jnp.logical_and(lane >= n * FOX_HEADS, lane < (n + 1) * FOX_HEADS)
        packed = jnp.where(in_group, pltpu.roll(piece, n * FOX_HEADS, 1) if n else piece, packed)
    for h in range(FOX_HEADS):
        placed = pltpu.roll(packed, (_bias_lanes(h)[0] - h) % LANES, 1)
        bias = jnp.where(_is_bias_lane(lane, h), placed, 0.0).astype(BF16)
        kpair = k_ref[0, :, (h // 2) * LANES:(h // 2 + 1) * LANES]
        o_ref[0, :, h * LANES:(h + 1) * LANES] = jnp.where(_half_mask(kpair.shape, h % 2), kpair, bias)


def _fox_keys(k, kb, tm=512):
    b, t, _ = k.shape
    tm = _largest_tile(t, (tm, 384, 256, 128))
    spec = lambda w: pl.BlockSpec((1, tm, w), lambda bb, i: (bb, i, 0))
    return pl.pallas_call(
        _fox_keys_kernel,
        grid=(b, t // tm),
        in_specs=[spec(FOX_WIDTH), spec(LANES)],
        out_specs=spec(FOX_HEADS * LANES),
        out_shape=jax.ShapeDtypeStruct((b, t, FOX_HEADS * LANES), BF16),
        compiler_params=pltpu.CompilerParams(dimension_semantics=("parallel", "parallel")),
        name="fox_keys",
    )(k, kb)


def _fox_kernel(q_ref, k_ref, vt_ref, o_ref, qm_ref, s_ref, st_ref, m_ref, acc_ref, *, tq, tk, q_pos0, nk):
    row0 = q_pos0 + pl.program_id(1) * tq
    n_full = jnp.minimum(nk, (row0 + 1) // tk)
    j_end = jnp.minimum(nk, (row0 + tq - 1) // tk + 1)
    _mask_heads(q_ref, qm_ref, FOX_HEADS, bias_ones=True)
    _init_softmax(m_ref, acc_ref)

    def logits(j, masked):
        off = pl.multiple_of(j * tk, tk)
        if masked:
            key_pos = j * tk + lax.broadcasted_iota(jnp.int32, (tk, tq), 0)
            visible = key_pos <= row0 + lax.broadcasted_iota(jnp.int32, (tk, tq), 1)
        for h in range(FOX_HEADS):
            s = _nt_dot(k_ref[0, pl.ds(off, tk), h * LANES:(h + 1) * LANES], qm_ref[h])
            yield jnp.where(visible, s, NEG_INF) if masked else s

    def values(j, h):
        return vt_ref[0, j, h * VAL_ROWS:(h + 1) * VAL_ROWS, :]

    masked = functools.partial(logits, masked=True)
    _flash_loop([(n_full, functools.partial(logits, masked=False)), (j_end, masked)], masked, values,
                s_ref, st_ref, m_ref, acc_ref)
    _store_heads(o_ref, acc_ref, FOX_HEADS)


def _fox_attn(q, k, v, kb, *, q_pos0, tq, tk):
    b, tq_all, _ = q.shape
    tk_all = k.shape[1]
    assert tq_all % tq == 0 and tk_all % tk == 0 and tq % LANES == 0
    nk = tk_all // tk
    vt = _key_major(v, tk)
    k_aug = _fox_keys(k, kb)
    return pl.pallas_call(
        functools.partial(_fox_kernel, tq=tq, tk=tk, q_pos0=q_pos0, nk=nk),
        grid=(b, tq_all // tq),
        in_specs=[pl.BlockSpec((1, tq, FOX_WIDTH), lambda bb, i: (bb, i, 0)),
                  pl.BlockSpec((1, tk_all, FOX_HEADS * LANES), lambda bb, i: (bb, 0, 0)),
                  pl.BlockSpec((1, nk, FOX_HEADS * VAL_ROWS, tk), lambda bb, i: (bb, 0, 0, 0))],
        out_specs=pl.BlockSpec((1, tq, FOX_WIDTH), lambda bb, i: (bb, i, 0)),
        out_shape=jax.ShapeDtypeStruct(q.shape, BF16),
        scratch_shapes=[pltpu.VMEM((FOX_HEADS, tq, LANES), BF16)] + _flash_scratch(FOX_HEADS, tq, tk),
        compiler_params=pltpu.CompilerParams(dimension_semantics=("parallel", "arbitrary"),
                                             vmem_limit_bytes=VMEM_LIMIT),
        name="fox_attn",
    )(q, k_aug, vt)


def _dsa_kernel(dq_ref, iq_ref, w_ref, dk_ref, dvt_ref, ik_ref, o_ref,
                qm_ref, iqm_ref, key_ref, hi_ref, lo_ref, drop_ref, s_ref, st_ref, m_ref, acc_ref,
                *, tq, tk, q_pos0, n_real, top_k, nk):
    row0 = q_pos0 + pl.program_id(1) * tq
    q_pos = row0 + lax.broadcasted_iota(jnp.int32, (1, tq), 1)
    lim = jnp.minimum(lax.shift_left(jnp.right_shift(q_pos, CHUNK_SHIFT) + 1, CHUNK_SHIFT), n_real)
    lim_first = jnp.minimum((row0 // CHUNK + 1) * CHUNK, n_real)
    lim_last = jnp.minimum(((row0 + tq - 1) // CHUNK + 1) * CHUNK, n_real)
    n_full = lim_first // tk
    nt = (lim_last + tk - 1) // tk
    _mask_heads(dq_ref, qm_ref, DSA_HEADS)
    _mask_heads(iq_ref, iqm_ref, IDX_HEADS)

    def ktile(ref, j, lo, hi):
        return ref[0, pl.ds(pl.multiple_of(j * tk, tk), tk), lo:hi]

    def key_pos(j):
        return j * tk + lax.broadcasted_iota(jnp.int32, (tk, tq), 0)

    def score_tile(j, edge):
        ikt = ktile(ik_ref, j, 0, LANES)
        sc = jnp.zeros((tk, tq), F32)
        for h in range(IDX_HEADS):
            sc = sc + w_ref[0, h:h + 1, :] * jnp.maximum(_nt_dot(ikt, iqm_ref[h]), 0.0)
        bits = pltpu.bitcast(sc, jnp.int32)
        key = jnp.where(bits < 0, (bits ^ 0x7FFFFFFF) + 1 - ZERO_BAND, bits)
        pos = key_pos(j)
        key = jnp.where(sc == 0.0, -1 - pos, key)
        if edge:
            key = jnp.where(pos < lim, key, INT_MIN)
        key_ref[j] = key
        hi_ref[j] = jnp.right_shift(key, 16).astype(jnp.int16)
        lo_ref[j] = (jnp.bitwise_and(key, 0xFFFF) - 32768).astype(jnp.int16)

    def score_full(j, carry):
        score_tile(j, False)
        return carry

    def score_edge(j, carry):
        score_tile(j, True)
        return carry

    lax.fori_loop(0, n_full, score_full, 0)
    lax.fori_loop(n_full, nt, score_edge, 0)

    def count16(ref, cand):
        cand16 = cand.astype(jnp.int16)

        def body(j, acc):
            hit = jnp.where(ref[j] >= cand16, jnp.asarray(1, BF16), jnp.asarray(0, BF16))
            part = hit[0:16, :]
            for g in range(1, tk // 16):
                part = part + hit[g * 16:(g + 1) * 16, :]
            return acc + part.astype(F32)

        acc = lax.fori_loop(0, nt, body, jnp.zeros((16, tq), F32))
        return jnp.sum(acc, axis=0, keepdims=True)

    def bisect16(ref, base, cnt0):
        def step(it, carry):
            t, cnt_t = carry
            cand = t + lax.shift_left(jnp.int32(1), 15 - it)
            cnt = base + count16(ref, cand)
            ok = cnt >= top_k
            return jnp.where(ok, cand, t), jnp.where(ok, cnt, cnt_t)
        return lax.fori_loop(0, 16, step, (jnp.full((1, tq), -32768, jnp.int32), cnt0))

    t_hi, cnt_hi = bisect16(hi_ref, 0.0, jnp.full((1, tq), 2.0 ** 30, F32))
    above = jnp.where(t_hi >= 32767, 0.0, count16(hi_ref, jnp.minimum(t_hi + 1, 32767)))
    t_hi16 = t_hi.astype(jnp.int16)

    def keep_low(j, carry):
        lo_ref[j] = jnp.where(hi_ref[j] == t_hi16, lo_ref[j], jnp.asarray(-32768, jnp.int16))
        return carry

    lax.fori_loop(0, nt, keep_low, 0)
    t_lo, cnt_thr = bisect16(lo_ref, above, cnt_hi)
    thr = t_hi * 65536 + (t_lo + 32768)

    def count(pred):
        def body(j, acc):
            hit = jnp.where(pred(key_ref[j], j), 1.0, 0.0)
            return acc + jnp.sum(hit.reshape(tk // SUBLANES, SUBLANES, tq), axis=0)
        acc = lax.fori_loop(0, nt, body, jnp.zeros((SUBLANES, tq), F32))
        return jnp.sum(acc, axis=0, keepdims=True)

    excess = jnp.logical_and(cnt_thr > top_k, thr > INT_MIN)

    @pl.when(jnp.max(jnp.where(excess, 1.0, 0.0)) > 0.0)
    def _():
        need = top_k - count(lambda kt, j: kt > thr)
        bound = jnp.zeros((1, tq), jnp.int32)
        for bit in reversed(range(int(nk * tk - 1).bit_length())):
            cand = bound + (1 << bit)
            c = count(lambda kt, j: jnp.logical_and(kt == thr, key_pos(j) < cand))
            bound = jnp.where(c < need, cand, bound)

        def demote(j, carry):
            kt = key_ref[j]
            drop = jnp.logical_and(jnp.logical_and(kt == thr, key_pos(j) > bound), excess)
            key_ref[j] = jnp.where(drop, kt - 1, kt)
            return carry

        lax.fori_loop(0, nt, demote, 0)

    thr_sel = jnp.maximum(thr, INT_MIN + 1)
    _init_softmax(m_ref, acc_ref)
    kv_head = lambda h: h // (DSA_HEADS // DSA_KV_HEADS)

    def logits(j):
        drop_ref[...] = jnp.where(key_ref[j] >= thr_sel, 0.0, NEG_INF)
        for h in range(DSA_HEADS):
            g = kv_head(h)
            yield _nt_dot(ktile(dk_ref, j, g * LANES, (g + 1) * LANES), qm_ref[h]) + drop_ref[...]

    def values(j, h):
        return dvt_ref[0, j, kv_head(h) * VAL_ROWS:(kv_head(h) + 1) * VAL_ROWS, :]

    _flash_loop([(nt, logits)], logits, values, s_ref, st_ref, m_ref, acc_ref)
    _store_heads(o_ref, acc_ref, DSA_HEADS)


def _dsa_attn(dq, iq, w_t, dk2, dv, ik2, *, q_pos0, n_real, top_k, tq, tk):
    b, tq_all, _ = dq.shape
    tk_all = dk2.shape[1]
    assert tq_all % tq == 0 and tk_all % tk == 0 and tq % LANES == 0 and tk % 16 == 0 and tk_all <= ZERO_BAND
    nk = tk_all // tk
    dvt = _key_major(dv, tk)
    qspec = lambda w: pl.BlockSpec((1, tq, w), lambda bb, i: (bb, i, 0))
    kspec = lambda w: pl.BlockSpec((1, tk_all, w), lambda bb, i: (bb, 0, 0))
    return pl.pallas_call(
        functools.partial(_dsa_kernel, tq=tq, tk=tk, q_pos0=q_pos0, n_real=n_real, top_k=top_k, nk=nk),
        grid=(b, tq_all // tq),
        in_specs=[qspec(DSA_WIDTH), qspec(IDX_HEADS * HEAD_DIM),
                  pl.BlockSpec((1, SUBLANES, tq), lambda bb, i: (bb, 0, i)),
                  kspec(2 * LANES),
                  pl.BlockSpec((1, nk, DSA_KV_HEADS * VAL_ROWS, tk), lambda bb, i: (bb, 0, 0, 0)),
                  kspec(LANES)],
        out_specs=qspec(DSA_WIDTH),
        out_shape=jax.ShapeDtypeStruct(dq.shape, BF16),
        scratch_shapes=[pltpu.VMEM((DSA_HEADS, tq, LANES), BF16), pltpu.VMEM((IDX_HEADS, tq, LANES), BF16),
                        pltpu.VMEM((nk, tk, tq), jnp.int32), pltpu.VMEM((nk, tk, tq), jnp.int16),
                        pltpu.VMEM((nk, tk, tq), jnp.int16), pltpu.VMEM((tk, tq), F32)]
                       + _flash_scratch(DSA_HEADS, tq, tk),
        compiler_params=pltpu.CompilerParams(dimension_semantics=("parallel", "arbitrary"),
                                             vmem_limit_bytes=VMEM_LIMIT),
        name="dsa_attn",
    )(dq, iq, w_t, dk2, dvt, ik2)


def _layer_norm(x, g, b):
    mu = jnp.mean(x, axis=-1, keepdims=True)
    xc = x - mu
    var = jnp.mean(xc * xc, axis=-1, keepdims=True)
    return xc * lax.rsqrt(var + LN_EPS) * g + b


def _post_kernel(x_ref, fox_ref, dsa_ref, wo_ref, g1_ref, b1_ref, wg_ref, wu_ref, wd_ref, g2_ref, b2_ref,
                 o_ref, *, alpha, ff_chunk):
    mix = (jnp.dot(fox_ref[...], wo_ref[:FOX_WIDTH, :], preferred_element_type=F32)
           + jnp.dot(dsa_ref[...], wo_ref[FOX_WIDTH:, :], preferred_element_type=F32))
    h = _layer_norm(alpha * x_ref[...] + mix, g1_ref[...], b1_ref[...])
    hb = h.astype(BF16)
    f = jnp.zeros_like(h)
    for c in range(wg_ref.shape[1] // ff_chunk):
        lo, hi = c * ff_chunk, (c + 1) * ff_chunk
        gate = jnp.dot(hb, wg_ref[:, lo:hi], preferred_element_type=F32)
        up = jnp.dot(hb, wu_ref[:, lo:hi], preferred_element_type=F32)
        act = (gate * jax.nn.sigmoid(gate) * up).astype(BF16)
        f = f + jnp.dot(act, wd_ref[lo:hi, :], preferred_element_type=F32)
    o_ref[...] = _layer_norm(alpha * h + f, g2_ref[...], b2_ref[...])


def _post(x2, fox_o, dsa_o, w_out, ln1_g, ln1_b, w_gate, w_up, w_down, ln2_g, ln2_b, *, alpha, tm, ff_chunk=256):
    n, d = x2.shape
    d_ff = w_gate.shape[1]
    assert n % tm == 0 and d_ff % ff_chunk == 0
    row = lambda t: (t, 0)
    const = lambda t: (0, 0)
    wspec = lambda shape: pl.BlockSpec(shape, const, pipeline_mode=pl.Buffered(1))
    vec = lambda a: a.astype(F32)[None, :]
    return pl.pallas_call(
        functools.partial(_post_kernel, alpha=alpha, ff_chunk=ff_chunk),
        grid=(n // tm,),
        in_specs=[pl.BlockSpec((tm, d), row), pl.BlockSpec((tm, FOX_WIDTH), row), pl.BlockSpec((tm, DSA_WIDTH), row),
                  wspec(w_out.shape), wspec((1, d)), wspec((1, d)),
                  wspec(w_gate.shape), wspec(w_up.shape), wspec(w_down.shape), wspec((1, d)), wspec((1, d))],
        out_specs=pl.BlockSpec((tm, d), row),
        out_shape=jax.ShapeDtypeStruct((n, d), F32),
        compiler_params=pltpu.CompilerParams(dimension_semantics=("parallel",), vmem_limit_bytes=VMEM_LIMIT),
        name="post",
    )(x2, fox_o, dsa_o, w_out.astype(BF16), vec(ln1_g), vec(ln1_b),
      w_gate.astype(BF16), w_up.astype(BF16), w_down.astype(BF16), vec(ln2_g), vec(ln2_b))


def _round_up(n, m):
    return -(-n // m) * m


def _pad_rows(a, rows):
    return jnp.pad(a, ((0, 0), (0, rows - a.shape[1]), (0, 0)))


def _largest_tile(n, candidates):
    return next(c for c in candidates if n % c == 0)


def _mixer_layer(y, past, weights, *, alpha):
    (w_packed, bias, w_out, ln1_g, ln1_b, w_gate, w_up, w_down, ln2_g, ln2_b) = weights
    b, t, d = y.shape
    p_len = 0 if past is None else past[0].shape[1]
    tables = _rope_lane_tables(p_len + jnp.arange(t))
    (fq, fk, fkb, fv, fvb, dq, dk, dv, dk2, dvb, iq, ik, ik2, misc, logf) = _project(
        y.reshape(b * t, d), w_packed, bias, tables, t, min(512, t))
    r3 = lambda a: a.reshape(b, t, a.shape[-1])
    fq, fkb, fvb, dq, dk2, dvb, iq, ik2, misc3 = map(r3, (fq, fkb, fvb, dq, dk2, dvb, iq, ik2, misc))

    t_pad = _round_up(t, LANES)
    neg_c_new = _cumsum(_pad_rows(misc3, t_pad), scale=-LOG2E)[:, :t]
    n_keys = p_len + t
    top_k = min(IDX_TOPK_MAX, n_keys // 4)
    if past is None:
        n_pad = n_keys
        k_all, v_all, kb = fkb, fvb, neg_c_new
        dk_all, dv_all, ik_all = dk2, dvb, ik2
    else:
        cf_k, cf_v, cf_logf, cd_k, cd_v, ci_k = past
        n_pad = _round_up(n_keys, LANES)
        cat = lambda old, new: _pad_rows(jnp.concatenate([old.astype(new.dtype), new], axis=1), n_pad)
        k_all = cat(cf_k.reshape(b, p_len, FOX_WIDTH), fkb)
        v_all = cat(cf_v.reshape(b, p_len, FOX_WIDTH), fvb)
        dk_all = cat(jnp.repeat(cd_k, 2, axis=2).reshape(b, p_len, 2 * LANES), dk2)
        dv_all = cat(cd_v.reshape(b, p_len, LANES), dvb)
        ik_all = cat(jnp.concatenate([ci_k, ci_k], axis=-1), ik2)
        lf_past = jnp.pad(cf_logf.astype(F32), ((0, 0), (0, 0), (0, LANES - FOX_HEADS)))
        d_past = _cumsum(lf_past, reverse=True, inclusive=False, scale=LOG2E)
        kb = cat(d_past, neg_c_new)
    tq = min(512, t_pad)
    tk = _largest_tile(n_pad, (512, 384, 256, 128) if tq >= 512 else (1408, 1024, 512, 384, 256, 128))
    pad_q = lambda a: _pad_rows(a, t_pad)
    w_t = jnp.transpose(misc3[:, :, _MISC_IW:_MISC_IW + SUBLANES], (0, 2, 1))
    w_t = jnp.pad(w_t, ((0, 0), (0, 0), (0, t_pad - t)))
    fox_o = _fox_attn(pad_q(fq), k_all, v_all, kb, q_pos0=p_len, tq=tq, tk=tk)[:, :t]
    dsa_o = _dsa_attn(pad_q(dq), pad_q(iq), w_t, dk_all, dv_all, ik_all, q_pos0=p_len, n_real=n_keys,
                      top_k=top_k, tq=tq, tk=tk)[:, :t]
    y_out = _post(y.reshape(b * t, d), fox_o.reshape(b * t, -1), dsa_o.reshape(b * t, -1),
                  w_out, ln1_g, ln1_b, w_gate, w_up, w_down, ln2_g, ln2_b, alpha=alpha, tm=min(512, b * t))
    states = (fk.reshape(b, t, FOX_HEADS, HEAD_DIM), fv.reshape(b, t, FOX_HEADS, HEAD_DIM),
              logf.reshape(b, t, FOX_HEADS),
              dk.reshape(b, t, DSA_KV_HEADS, HEAD_DIM), dv.reshape(b, t, DSA_KV_HEADS, HEAD_DIM),
              ik.reshape(b, t, HEAD_DIM))
    return y_out.reshape(b, t, d), states


def kernel(x_prompt, x_sample, cache_fox_k, cache_fox_v, cache_fox_logf, cache_dsa_k, cache_dsa_v, cache_idx_k,
           w_in, b_forget, w_out, ln1_g, ln1_b, w_gate, w_up, w_down, ln2_g, ln2_b):
    depth = w_in.shape[0]
    alpha = (2 * depth) ** 0.25
    yp, ys = x_prompt, x_sample
    p_states, s_states = [], []
    for l in range(depth):
        w_packed, bias = _pack_w_in(w_in[l], b_forget[l])
        weights = (w_packed, bias, w_out[l], ln1_g[l], ln1_b[l], w_gate[l], w_up[l], w_down[l], ln2_g[l], ln2_b[l])
        yp, st = _mixer_layer(yp, None, weights, alpha=alpha)
        p_states.append(st)
        past = (cache_fox_k[l], cache_fox_v[l], cache_fox_logf[l], cache_dsa_k[l], cache_dsa_v[l], cache_idx_k[l])
        ys, st = _mixer_layer(ys, past, weights, alpha=alpha)
        s_states.append(st)
    stack = lambda sts: tuple(jnp.stack([s[n] for s in sts], axis=0) for n in range(6))
    return (yp, ys) + stack(p_states) + stack(s_states)
```

```python
import functools

import numpy as np
import jax
import jax.numpy as jnp
from jax import lax
from jax.experimental import pallas as pl
from jax.experimental.pallas import tpu as pltpu

HEAD_DIM = 64
FOX_HEADS = 8
DSA_HEADS = 8
DSA_KV_HEADS = 2
IDX_HEADS = 4
CHUNK = 64
CHUNK_SHIFT = 6
IDX_TOPK_MAX = 256
ROPE_THETA = 500000.0
ROT_DIM = HEAD_DIM // 4
LN_EPS = 1e-5
LOG2E = 1.4426950408889634

LANES = 128
SUBLANES = 8
FOX_WIDTH = FOX_HEADS * HEAD_DIM
DSA_WIDTH = DSA_HEADS * HEAD_DIM
INT_MIN = -(2 ** 31)
ZERO_BAND = 1 << 13
NEG_INF = float("-inf")
F32 = jnp.float32
BF16 = jnp.bfloat16
VMEM_LIMIT = 56 * 1024 * 1024

_C_FQ, _C_FK, _C_FV, _C_DQ = 0, 512, 1024, 1536
_C_DK, _C_DV, _C_IQ, _C_IK, _C_MISC, _C_END = 2048, 2176, 2304, 2560, 2688, 2816
_MISC_IW = 8


def _nt_dot(a, b):
    return lax.dot_general(a, b, (((1,), (1,)), ((), ())), preferred_element_type=F32)


def _half_mask(shape, hh):
    lane = lax.broadcasted_iota(jnp.int32, shape, len(shape) - 1)
    return (lane < HEAD_DIM) if hh == 0 else (lane >= HEAD_DIM)


def _project_kernel(x_ref, w_ref, bias_ref, c_ref, sa_ref, sb_ref,
                    fq_ref, fk_ref, fkb_ref, fv_ref, fvb_ref, dq_ref, dk_ref, dv_ref,
                    dk2_ref, dvb_ref, iq_ref, ik_ref, ik2_ref, misc_ref, logf_ref):
    xb = x_ref[...].astype(BF16)
    cos, sa, sb = c_ref[...], sa_ref[...], sb_ref[...]

    def mm(lo, hi):
        return jnp.dot(xb, w_ref[:, lo:hi], preferred_element_type=F32)

    def rope(yb):
        return yb * cos + pltpu.roll(yb, LANES - ROT_DIM // 2, 1) * sa + pltpu.roll(yb, ROT_DIM // 2, 1) * sb

    def blocks(y):
        return [y[:, b * LANES:(b + 1) * LANES] for b in range(y.shape[1] // LANES)]

    fq_ref[...] = mm(_C_FQ, _C_FK).astype(BF16)
    y = mm(_C_FK, _C_FV)
    fk_ref[...] = y
    fkb_ref[...] = y.astype(BF16)
    y = mm(_C_FV, _C_DQ)
    fv_ref[...] = y
    fvb_ref[...] = y.astype(BF16)
    for b, yb in enumerate(blocks(mm(_C_DQ, _C_DK))):
        dq_ref[:, b * LANES:(b + 1) * LANES] = rope(yb).astype(BF16)

    half0 = _half_mask((x_ref.shape[0], LANES), 0)
    kv = mm(_C_DK, _C_IQ)
    k = rope(kv[:, :LANES])
    v = kv[:, LANES:]
    for src, dst in ((k, dk_ref), (v, dv_ref)):
        for g in range(DSA_KV_HEADS):
            dst[:, g, :] = src[:, g * HEAD_DIM:(g + 1) * HEAD_DIM]
    dvb_ref[...] = v.astype(BF16)
    sw = pltpu.roll(k, HEAD_DIM, 1)
    dk2_ref[:, :LANES] = jnp.where(half0, k, sw).astype(BF16)
    dk2_ref[:, LANES:] = jnp.where(half0, sw, k).astype(BF16)

    for b, yb in enumerate(blocks(mm(_C_IQ, _C_IK))):
        iq_ref[:, b * LANES:(b + 1) * LANES] = rope(yb).astype(BF16)
    ik = rope(mm(_C_IK, _C_MISC))
    ik_ref[...] = ik[:, :HEAD_DIM]
    ik2_ref[...] = ik.astype(BF16)

    z = mm(_C_MISC, _C_END)
    zf = z + bias_ref[...]
    logf = jnp.minimum(zf, 0.0) - jnp.log1p(jnp.exp(-jnp.abs(zf)))
    lane = lax.broadcasted_iota(jnp.int32, z.shape, 1)
    misc_ref[...] = jnp.where(lane < _MISC_IW, logf, z)
    logf_ref[...] = logf[:, :FOX_HEADS]


def _pack_w_in(w, b_forget):
    d = w.shape[0]
    sizes = (FOX_WIDTH, FOX_WIDTH, FOX_WIDTH, FOX_HEADS, DSA_WIDTH, DSA_KV_HEADS * HEAD_DIM,
             DSA_KV_HEADS * HEAD_DIM, IDX_HEADS * HEAD_DIM, HEAD_DIM, IDX_HEADS)
    offs = np.concatenate([[0], np.cumsum(sizes)])
    fq, fk, fv, ff, dq, dk, dv, iq, ik, iw = [w[:, offs[i]:offs[i + 1]] for i in range(len(sizes))]
    scale = HEAD_DIM ** -0.5 * LOG2E
    misc = jnp.concatenate([ff, iw, jnp.zeros((d, LANES - FOX_HEADS - IDX_HEADS), w.dtype)], axis=1)
    packed = jnp.concatenate([fq * scale, fk, fv, dq * scale, dk, dv, iq, ik, ik, misc], axis=1)
    bias = jnp.concatenate([b_forget.astype(F32), jnp.zeros((LANES - FOX_HEADS,), F32)])[None, :]
    return packed.astype(BF16), bias


def _rope_lane_tables(positions):
    half = ROT_DIM // 2
    inv_freq = ROPE_THETA ** (-jnp.arange(half, dtype=F32) * 2.0 / ROT_DIM)
    ang = positions.astype(F32)[:, None] * inv_freq[None, :]
    cos, sin = jnp.cos(ang), jnp.sin(ang)
    n = positions.shape[0]
    rest = HEAD_DIM - ROT_DIM
    c = jnp.concatenate([cos, cos, jnp.ones((n, rest), F32)], axis=1)
    sa = jnp.concatenate([-sin, jnp.zeros((n, half + rest), F32)], axis=1)
    sb = jnp.concatenate([jnp.zeros((n, half), F32), sin, jnp.zeros((n, rest), F32)], axis=1)
    rep = LANES // HEAD_DIM
    return tuple(jnp.tile(t, (1, rep)) for t in (c, sa, sb))


def _project(x2, w_packed, bias, tables, seq, tm):
    n, d = x2.shape
    assert seq % tm == 0 and n % seq == 0
    tiles_per_seq = seq // tm
    row = lambda t: (t, 0)
    pos = lambda t: (t % tiles_per_seq, 0)
    const = lambda t: (0, 0)

    def out(width, dtype):
        return jax.ShapeDtypeStruct((n, width), dtype), pl.BlockSpec((tm, width), row)

    kv_state = (jax.ShapeDtypeStruct((n, DSA_KV_HEADS, HEAD_DIM), F32),
                pl.BlockSpec((tm, DSA_KV_HEADS, HEAD_DIM), lambda t: (t, 0, 0)))
    outs = [out(FOX_WIDTH, BF16),
            out(FOX_WIDTH, F32), out(FOX_WIDTH, BF16),
            out(FOX_WIDTH, F32), out(FOX_WIDTH, BF16),
            out(DSA_WIDTH, BF16),
            kv_state, kv_state,
            out(2 * LANES, BF16), out(LANES, BF16),
            out(IDX_HEADS * HEAD_DIM, BF16),
            out(HEAD_DIM, F32), out(LANES, BF16),
            out(LANES, F32),
            out(FOX_HEADS, F32)]
    return pl.pallas_call(
        _project_kernel,
        grid=(n // tm,),
        in_specs=[pl.BlockSpec((tm, d), row),
                  pl.BlockSpec((d, _C_END), const, pipeline_mode=pl.Buffered(1)),
                  pl.BlockSpec((1, LANES), const),
                  pl.BlockSpec((tm, LANES), pos), pl.BlockSpec((tm, LANES), pos), pl.BlockSpec((tm, LANES), pos)],
        out_specs=[o[1] for o in outs],
        out_shape=[o[0] for o in outs],
        compiler_params=pltpu.CompilerParams(dimension_semantics=("parallel",), vmem_limit_bytes=VMEM_LIMIT),
        name="project",
    )(x2, w_packed, bias, *tables)


def _cumsum_kernel(x_ref, o_ref, carry_ref, *, reverse, inclusive, scale):
    @pl.when(pl.program_id(1) == 0)
    def _():
        carry_ref[...] = jnp.zeros_like(carry_ref)

    x = x_ref[0]
    tm = x.shape[0]
    hi = x.astype(BF16)
    r1 = x - hi.astype(F32)
    mid = r1.astype(BF16)
    lo = (r1 - mid.astype(F32)).astype(BF16)
    row = lax.broadcasted_iota(jnp.int32, (tm, tm), 0)
    col = lax.broadcasted_iota(jnp.int32, (tm, tm), 1)
    if reverse:
        keep = (col >= row) if inclusive else (col > row)
    else:
        keep = (col <= row) if inclusive else (col < row)
    tri = jnp.where(keep, 1.0, 0.0).astype(BF16)
    cs = (jnp.dot(tri, hi, preferred_element_type=F32) + jnp.dot(tri, mid, preferred_element_type=F32)
          + jnp.dot(tri, lo, preferred_element_type=F32))
    carry = carry_ref[0:1, :]
    o_ref[0] = scale * (cs + carry)
    carry_ref[0:1, :] = carry + jnp.sum(x, axis=0, keepdims=True)


def _cumsum(x, *, reverse=False, inclusive=True, scale=1.0, tm=512):
    b, s, w = x.shape
    tm = min(tm, s)
    assert s % tm == 0 and w == LANES
    nt = s // tm
    idx = (lambda i, t: (i, nt - 1 - t, 0)) if reverse else (lambda i, t: (i, t, 0))
    return pl.pallas_call(
        functools.partial(_cumsum_kernel, reverse=reverse, inclusive=inclusive, scale=scale),
        grid=(b, nt),
        in_specs=[pl.BlockSpec((1, tm, w), idx)],
        out_specs=pl.BlockSpec((1, tm, w), idx),
        out_shape=jax.ShapeDtypeStruct(x.shape, F32),
        scratch_shapes=[pltpu.VMEM((SUBLANES, w), F32)],
        compiler_params=pltpu.CompilerParams(dimension_semantics=("parallel", "arbitrary")),
        name="cumsum",
    )(x)


BIAS_PIECES = 3


def _bias_lanes(h):
    base = HEAD_DIM if h % 2 == 0 else 0
    return tuple(base + n * FOX_HEADS for n in range(BIAS_PIECES))


def _is_bias_lane(lane, h):
    hit = lane == _bias_lanes(h)[0]
    for dst in _bias_lanes(h)[1:]:
        hit = jnp.logical_or(hit, lane == dst)
    return hit


def _mask_heads(q_ref, qm_ref, n_heads, bias_ones=False):
    for h in range(n_heads):
        qpair = q_ref[0, :, (h // 2) * LANES:(h // 2 + 1) * LANES]
        other = jnp.zeros_like(qpair)
        if bias_ones:
            lane = lax.broadcasted_iota(jnp.int32, qpair.shape, 1)
            other = jnp.where(_is_bias_lane(lane, h), 1.0, 0.0).astype(qpair.dtype)
        qm_ref[h] = jnp.where(_half_mask(qpair.shape, h % 2), qpair, other)


ONES_ROWS = 16
VAL_ROWS = HEAD_DIM + ONES_ROWS


def _flash_scratch(n_heads, tq, tk):
    return [pltpu.VMEM((2, n_heads, tk, tq), F32), pltpu.VMEM((2, n_heads, SUBLANES, tq), F32),
            pltpu.VMEM((n_heads, 1, tq), F32), pltpu.VMEM((n_heads, VAL_ROWS, tq), F32)]


def _init_softmax(m_ref, acc_ref):
    m_ref[...] = jnp.full_like(m_ref, NEG_INF)
    acc_ref[...] = jnp.zeros_like(acc_ref)


def _flash_loop(segments, logits0, values, s_ref, st_ref, m_ref, acc_ref):
    def by_parity(j, fn):
        for slot in range(2):
            pl.when(lax.rem(j, 2) == slot)(functools.partial(fn, slot))

    n_heads = m_ref.shape[0]

    def stage_a(h, s, slot):
        s_ref[slot, h] = s
        m_prev = m_ref[h]
        m_new = jnp.maximum(m_prev, jnp.max(s, axis=0, keepdims=True))
        m_use = jnp.where(m_new == NEG_INF, 0.0, m_new)
        st_ref[slot, h, 0:1, :] = m_use
        st_ref[slot, h, 1:2, :] = jnp.exp2(m_prev - m_use)
        m_ref[h] = m_new

    def stage_b(j, h, slot):
        p = jnp.exp2(s_ref[slot, h] - st_ref[slot, h, 0:1, :]).astype(BF16)
        alpha = st_ref[slot, h, 1:2, :]
        acc_ref[h] = alpha * acc_ref[h] + jnp.dot(values(j, h), p, preferred_element_type=F32)

    for h, s in enumerate(logits0(0)):
        stage_a(h, s, 0)
    start = 1
    for end, logits in segments:
        def both(slot, j, logits):
            tiles = logits(j)
            for h in range(n_heads):
                stage_b(j - 1, h, 1 - slot)
                stage_a(h, next(tiles), slot)

        def body(j, carry, logits=logits):
            by_parity(j, functools.partial(both, j=j, logits=logits))
            return carry

        lax.fori_loop(start, end, body, 0)
        start = jnp.maximum(start, end)
    def drain(slot):
        for h in range(n_heads):
            stage_b(start - 1, h, slot)

    by_parity(start - 1, drain)


def _store_heads(o_ref, acc_ref, n_heads):
    def head(h):
        return acc_ref[h, :HEAD_DIM, :] / acc_ref[h, HEAD_DIM:HEAD_DIM + 1, :]

    for c in range(n_heads // 2):
        ot = jnp.concatenate([head(2 * c), head(2 * c + 1)], axis=0)
        o_ref[0, :, c * LANES:(c + 1) * LANES] = ot.T.astype(o_ref.dtype)


def _key_major(v, tk):
    b, t, c = v.shape
    heads = c // HEAD_DIM
    vt = jnp.transpose(v.reshape(b, t // tk, tk, heads, HEAD_DIM), (0, 1, 3, 4, 2))
    ones = jnp.ones((b, t // tk, heads, ONES_ROWS, tk), v.dtype)
    return jnp.concatenate([vt, ones], axis=3).reshape(b, t // tk, heads * VAL_ROWS, tk)


def _fox_keys_kernel(k_ref, kb_ref, o_ref):
    kb = kb_ref[0]
    hi = kb.astype(BF16).astype(F32)
    r1 = kb - hi
    mid = r1.astype(BF16).astype(F32)
    lane = lax.broadcasted_iota(jnp.int32, kb.shape, 1)
    packed = jnp.zeros_like(kb)
    for n, piece in enumerate((hi, mid, r1 - mid)):
        in_group = jnp.logical_and(lane >= n * FOX_HEADS, lane < (n + 1) * FOX_HEADS)
        packed = jnp.where(in_group, pltpu.roll(piece, n * FOX_HEADS, 1) if n else piece, packed)
    for h in range(FOX_HEADS):
        placed = pltpu.roll(packed, (_bias_lanes(h)[0] - h) % LANES, 1)
        bias = jnp.where(_is_bias_lane(lane, h), placed, 0.0).astype(BF16)
        kpair = k_ref[0, :, (h // 2) * LANES:(h // 2 + 1) * LANES]
        o_ref[0, :, h * LANES:(h + 1) * LANES] = jnp.where(_half_mask(kpair.shape, h % 2), kpair, bias)


def _fox_keys(k, kb, tm=512):
    b, t, _ = k.shape
    tm = _largest_tile(t, (tm, 384, 256, 128))
    spec = lambda w: pl.BlockSpec((1, tm, w), lambda bb, i: (bb, i, 0))
    return pl.pallas_call(
        _fox_keys_kernel,
        grid=(b, t // tm),
        in_specs=[spec(FOX_WIDTH), spec(LANES)],
        out_specs=spec(FOX_HEADS * LANES),
        out_shape=jax.ShapeDtypeStruct((b, t, FOX_HEADS * LANES), BF16),
        compiler_params=pltpu.CompilerParams(dimension_semantics=("parallel", "parallel")),
        name="fox_keys",
    )(k, kb)


def _fox_kernel(q_ref, k_ref, vt_ref, o_ref, qm_ref, s_ref, st_ref, m_ref, acc_ref, *, tq, tk, q_pos0, nk):
    row0 = q_pos0 + pl.program_id(1) * tq
    n_full = jnp.minimum(nk, (row0 + 1) // tk)
    j_end = jnp.minimum(nk, (row0 + tq - 1) // tk + 1)
    _mask_heads(q_ref, qm_ref, FOX_HEADS, bias_ones=True)
    _init_softmax(m_ref, acc_ref)

    def logits(j, masked):
        off = pl.multiple_of(j * tk, tk)
        if masked:
            key_pos = j * tk + lax.broadcasted_iota(jnp.int32, (tk, tq), 0)
            visible = key_pos <= row0 + lax.broadcasted_iota(jnp.int32, (tk, tq), 1)
        for h in range(FOX_HEADS):
            s = _nt_dot(k_ref[0, pl.ds(off, tk), h * LANES:(h + 1) * LANES], qm_ref[h])
            yield jnp.where(visible, s, NEG_INF) if masked else s

    def values(j, h):
        return vt_ref[0, j, h * VAL_ROWS:(h + 1) * VAL_ROWS, :]

    masked = functools.partial(logits, masked=True)
    _flash_loop([(n_full, functools.partial(logits, masked=False)), (j_end, masked)], masked, values,
                s_ref, st_ref, m_ref, acc_ref)
    _store_heads(o_ref, acc_ref, FOX_HEADS)


def _fox_attn(q, k, v, kb, *, q_pos0, tq, tk):
    b, tq_all, _ = q.shape
    tk_all = k.shape[1]
    assert tq_all % tq == 0 and tk_all % tk == 0 and tq % LANES == 0
    nk = tk_all // tk
    vt = _key_major(v, tk)
    k_aug = _fox_keys(k, kb)
    return pl.pallas_call(
        functools.partial(_fox_kernel, tq=tq, tk=tk, q_pos0=q_pos0, nk=nk),
        grid=(b, tq_all // tq),
        in_specs=[pl.BlockSpec((1, tq, FOX_WIDTH), lambda bb, i: (bb, i, 0)),
                  pl.BlockSpec((1, tk_all, FOX_HEADS * LANES), lambda bb, i: (bb, 0, 0)),
                  pl.BlockSpec((1, nk, FOX_HEADS * VAL_ROWS, tk), lambda bb, i: (bb, 0, 0, 0))],
        out_specs=pl.BlockSpec((1, tq, FOX_WIDTH), lambda bb, i: (bb, i, 0)),
        out_shape=jax.ShapeDtypeStruct(q.shape, BF16),
        scratch_shapes=[pltpu.VMEM((FOX_HEADS, tq, LANES), BF16)] + _flash_scratch(FOX_HEADS, tq, tk),
        compiler_params=pltpu.CompilerParams(dimension_semantics=("parallel", "arbitrary"),
                                             vmem_limit_bytes=VMEM_LIMIT),
        name="fox_attn",
    )(q, k_aug, vt)


def _dsa_kernel(dq_ref, iq_ref, w_ref, dk_ref, dvt_ref, ik_ref, o_ref,
                qm_ref, iqm_ref, key_ref, hi_ref, lo_ref, drop_ref, s_ref, st_ref, m_ref, acc_ref,
                *, tq, tk, q_pos0, n_real, top_k, nk):
    row0 = q_pos0 + pl.program_id(1) * tq
    q_pos = row0 + lax.broadcasted_iota(jnp.int32, (1, tq), 1)
    lim = jnp.minimum(lax.shift_left(jnp.right_shift(q_pos, CHUNK_SHIFT) + 1, CHUNK_SHIFT), n_real)
    lim_first = jnp.minimum((row0 // CHUNK + 1) * CHUNK, n_real)
    lim_last = jnp.minimum(((row0 + tq - 1) // CHUNK + 1) * CHUNK, n_real)
    n_full = lim_first // tk
    nt = (lim_last + tk - 1) // tk
    _mask_heads(dq_ref, qm_ref, DSA_HEADS)
    _mask_heads(iq_ref, iqm_ref, IDX_HEADS)

    def ktile(ref, j, lo, hi):
        return ref[0, pl.ds(pl.multiple_of(j * tk, tk), tk), lo:hi]

    def key_pos(j):
        return j * tk + lax.broadcasted_iota(jnp.int32, (tk, tq), 0)

    def score_tile(j, edge):
        ikt = ktile(ik_ref, j, 0, LANES)
        sc = jnp.zeros((tk, tq), F32)
        for h in range(IDX_HEADS):
            sc = sc + w_ref[0, h:h + 1, :] * jnp.maximum(_nt_dot(ikt, iqm_ref[h]), 0.0)
        bits = pltpu.bitcast(sc, jnp.int32)
        key = jnp.where(bits < 0, (bits ^ 0x7FFFFFFF) + 1 - ZERO_BAND, bits)
        pos = key_pos(j)
        key = jnp.where(sc == 0.0, -1 - pos, key)
        if edge:
            key = jnp.where(pos < lim, key, INT_MIN)
        key_ref[j] = key
        hi_ref[j] = jnp.right_shift(key, 16).astype(jnp.int16)
        lo_ref[j] = (jnp.bitwise_and(key, 0xFFFF) - 32768).astype(jnp.int16)

    def score_full(j, carry):
        score_tile(j, False)
        return carry

    def score_edge(j, carry):
        score_tile(j, True)
        return carry

    lax.fori_loop(0, n_full, score_full, 0)
    lax.fori_loop(n_full, nt, score_edge, 0)

    def count16(ref, cand):
        cand16 = cand.astype(jnp.int16)

        def body(j, acc):
            hit = jnp.where(ref[j] >= cand16, jnp.asarray(1, BF16), jnp.asarray(0, BF16))
            n_acc = max(1, 4 * LANES // tq)
            parts = [hit[g * 16:(g + 1) * 16, :] for g in range(n_acc)]
            for g in range(n_acc, tk // 16):
                parts[g % n_acc] = parts[g % n_acc] + hit[g * 16:(g + 1) * 16, :]
            part = parts[0]
            for extra in parts[1:]:
                part = part + extra
            return acc + part.astype(F32)

        acc = lax.fori_loop(0, nt, body, jnp.zeros((16, tq), F32))
        return jnp.sum(acc, axis=0, keepdims=True)

    def bisect16(ref, base, cnt0):
        def step(it, carry):
            t, cnt_t = carry
            cand = t + lax.shift_left(jnp.int32(1), 15 - it)
            cnt = base + count16(ref, cand)
            ok = cnt >= top_k
            return jnp.where(ok, cand, t), jnp.where(ok, cnt, cnt_t)
        return lax.fori_loop(0, 16, step, (jnp.full((1, tq), -32768, jnp.int32), cnt0))

    t_hi, cnt_hi = bisect16(hi_ref, 0.0, jnp.full((1, tq), 2.0 ** 30, F32))
    above = jnp.where(t_hi >= 32767, 0.0, count16(hi_ref, jnp.minimum(t_hi + 1, 32767)))
    t_hi16 = t_hi.astype(jnp.int16)

    def keep_low(j, carry):
        lo_ref[j] = jnp.where(hi_ref[j] == t_hi16, lo_ref[j], jnp.asarray(-32768, jnp.int16))
        return carry

    lax.fori_loop(0, nt, keep_low, 0)
    t_lo, cnt_thr = bisect16(lo_ref, above, cnt_hi)
    thr = t_hi * 65536 + (t_lo + 32768)

    def count(pred):
        def body(j, acc):
            hit = jnp.where(pred(key_ref[j], j), 1.0, 0.0)
            return acc + jnp.sum(hit.reshape(tk // SUBLANES, SUBLANES, tq), axis=0)
        acc = lax.fori_loop(0, nt, body, jnp.zeros((SUBLANES, tq), F32))
        return jnp.sum(acc, axis=0, keepdims=True)

    excess = jnp.logical_and(cnt_thr > top_k, thr > INT_MIN)

    @pl.when(jnp.max(jnp.where(excess, 1.0, 0.0)) > 0.0)
    def _():
        need = top_k - count(lambda kt, j: kt > thr)
        bound = jnp.zeros((1, tq), jnp.int32)
        for bit in reversed(range(int(nk * tk - 1).bit_length())):
            cand = bound + (1 << bit)
            c = count(lambda kt, j: jnp.logical_and(kt == thr, key_pos(j) < cand))
            bound = jnp.where(c < need, cand, bound)

        def demote(j, carry):
            kt = key_ref[j]
            drop = jnp.logical_and(jnp.logical_and(kt == thr, key_pos(j) > bound), excess)
            key_ref[j] = jnp.where(drop, kt - 1, kt)
            return carry

        lax.fori_loop(0, nt, demote, 0)

    thr_sel = jnp.maximum(thr, INT_MIN + 1)
    _init_softmax(m_ref, acc_ref)
    kv_head = lambda h: h // (DSA_HEADS // DSA_KV_HEADS)

    def logits(j):
        drop_ref[...] = jnp.where(key_ref[j] >= thr_sel, 0.0, NEG_INF)
        for h in range(DSA_HEADS):
            g = kv_head(h)
            yield _nt_dot(ktile(dk_ref, j, g * LANES, (g + 1) * LANES), qm_ref[h]) + drop_ref[...]

    def values(j, h):
        return dvt_ref[0, j, kv_head(h) * VAL_ROWS:(kv_head(h) + 1) * VAL_ROWS, :]

    _flash_loop([(nt, logits)], logits, values, s_ref, st_ref, m_ref, acc_ref)
    _store_heads(o_ref, acc_ref, DSA_HEADS)


def _dsa_attn(dq, iq, w_t, dk2, dv, ik2, *, q_pos0, n_real, top_k, tq, tk):
    b, tq_all, _ = dq.shape
    tk_all = dk2.shape[1]
    assert tq_all % tq == 0 and tk_all % tk == 0 and tq % LANES == 0 and tk % 16 == 0 and tk_all <= ZERO_BAND
    nk = tk_all // tk
    dvt = _key_major(dv, tk)
    qspec = lambda w: pl.BlockSpec((1, tq, w), lambda bb, i: (bb, i, 0))
    kspec = lambda w: pl.BlockSpec((1, tk_all, w), lambda bb, i: (bb, 0, 0))
    return pl.pallas_call(
        functools.partial(_dsa_kernel, tq=tq, tk=tk, q_pos0=q_pos0, n_real=n_real, top_k=top_k, nk=nk),
        grid=(b, tq_all // tq),
        in_specs=[qspec(DSA_WIDTH), qspec(IDX_HEADS * HEAD_DIM),
                  pl.BlockSpec((1, SUBLANES, tq), lambda bb, i: (bb, 0, i)),
                  kspec(2 * LANES),
                  pl.BlockSpec((1, nk, DSA_KV_HEADS * VAL_ROWS, tk), lambda bb, i: (bb, 0, 0, 0)),
                  kspec(LANES)],
        out_specs=qspec(DSA_WIDTH),
        out_shape=jax.ShapeDtypeStruct(dq.shape, BF16),
        scratch_shapes=[pltpu.VMEM((DSA_HEADS, tq, LANES), BF16), pltpu.VMEM((IDX_HEADS, tq, LANES), BF16),
                        pltpu.VMEM((nk, tk, tq), jnp.int32), pltpu.VMEM((nk, tk, tq), jnp.int16),
                        pltpu.VMEM((nk, tk, tq), jnp.int16), pltpu.VMEM((tk, tq), F32)]
                       + _flash_scratch(DSA_HEADS, tq, tk),
        compiler_params=pltpu.CompilerParams(dimension_semantics=("parallel", "arbitrary"),
                                             vmem_limit_bytes=VMEM_LIMIT),
        name="dsa_attn",
    )(dq, iq, w_t, dk2, dvt, ik2)


def _layer_norm(x, g, b):
    mu = jnp.mean(x, axis=-1, keepdims=True)
    xc = x - mu
    var = jnp.mean(xc * xc, axis=-1, keepdims=True)
    return xc * lax.rsqrt(var + LN_EPS) * g + b


def _post_kernel(x_ref, fox_ref, dsa_ref, wo_ref, g1_ref, b1_ref, wg_ref, wu_ref, wd_ref, g2_ref, b2_ref,
                 o_ref, *, alpha, ff_chunk):
    mix = (jnp.dot(fox_ref[...], wo_ref[:FOX_WIDTH, :], preferred_element_type=F32)
           + jnp.dot(dsa_ref[...], wo_ref[FOX_WIDTH:, :], preferred_element_type=F32))
    h = _layer_norm(alpha * x_ref[...] + mix, g1_ref[...], b1_ref[...])
    hb = h.astype(BF16)
    f = jnp.zeros_like(h)
    for c in range(wg_ref.shape[1] // ff_chunk):
        lo, hi = c * ff_chunk, (c + 1) * ff_chunk
        gate = jnp.dot(hb, wg_ref[:, lo:hi], preferred_element_type=F32)
        up = jnp.dot(hb, wu_ref[:, lo:hi], preferred_element_type=F32)
        act = (gate * jax.nn.sigmoid(gate) * up).astype(BF16)
        f = f + jnp.dot(act, wd_ref[lo:hi, :], preferred_element_type=F32)
    o_ref[...] = _layer_norm(alpha * h + f, g2_ref[...], b2_ref[...])


def _post(x2, fox_o, dsa_o, w_out, ln1_g, ln1_b, w_gate, w_up, w_down, ln2_g, ln2_b, *, alpha, tm, ff_chunk=256):
    n, d = x2.shape
    d_ff = w_gate.shape[1]
    assert n % tm == 0 and d_ff % ff_chunk == 0
    row = lambda t: (t, 0)
    const = lambda t: (0, 0)
    wspec = lambda shape: pl.BlockSpec(shape, const, pipeline_mode=pl.Buffered(1))
    vec = lambda a: a.astype(F32)[None, :]
    return pl.pallas_call(
        functools.partial(_post_kernel, alpha=alpha, ff_chunk=ff_chunk),
        grid=(n // tm,),
        in_specs=[pl.BlockSpec((tm, d), row), pl.BlockSpec((tm, FOX_WIDTH), row), pl.BlockSpec((tm, DSA_WIDTH), row),
                  wspec(w_out.shape), wspec((1, d)), wspec((1, d)),
                  wspec(w_gate.shape), wspec(w_up.shape), wspec(w_down.shape), wspec((1, d)), wspec((1, d))],
        out_specs=pl.BlockSpec((tm, d), row),
        out_shape=jax.ShapeDtypeStruct((n, d), F32),
        compiler_params=pltpu.CompilerParams(dimension_semantics=("parallel",), vmem_limit_bytes=VMEM_LIMIT),
        name="post",
    )(x2, fox_o, dsa_o, w_out.astype(BF16), vec(ln1_g), vec(ln1_b),
      w_gate.astype(BF16), w_up.astype(BF16), w_down.astype(BF16), vec(ln2_g), vec(ln2_b))


def _round_up(n, m):
    return -(-n // m) * m


def _pad_rows(a, rows):
    return jnp.pad(a, ((0, 0), (0, rows - a.shape[1]), (0, 0)))


def _largest_tile(n, candidates):
    return next(c for c in candidates if n % c == 0)


def _mixer_layer(y, past, weights, *, alpha):
    (w_packed, bias, w_out, ln1_g, ln1_b, w_gate, w_up, w_down, ln2_g, ln2_b) = weights
    b, t, d = y.shape
    p_len = 0 if past is None else past[0].shape[1]
    tables = _rope_lane_tables(p_len + jnp.arange(t))
    (fq, fk, fkb, fv, fvb, dq, dk, dv, dk2, dvb, iq, ik, ik2, misc, logf) = _project(
        y.reshape(b * t, d), w_packed, bias, tables, t, min(512, t))
    r3 = lambda a: a.reshape(b, t, a.shape[-1])
    fq, fkb, fvb, dq, dk2, dvb, iq, ik2, misc3 = map(r3, (fq, fkb, fvb, dq, dk2, dvb, iq, ik2, misc))

    t_pad = _round_up(t, LANES)
    neg_c_new = _cumsum(_pad_rows(misc3, t_pad), scale=-LOG2E)[:, :t]
    n_keys = p_len + t
    top_k = min(IDX_TOPK_MAX, n_keys // 4)
    if past is None:
        n_pad = n_keys
        k_all, v_all, kb = fkb, fvb, neg_c_new
        dk_all, dv_all, ik_all = dk2, dvb, ik2
    else:
        cf_k, cf_v, cf_logf, cd_k, cd_v, ci_k = past
        n_pad = _round_up(n_keys, LANES)
        cat = lambda old, new: _pad_rows(jnp.concatenate([old.astype(new.dtype), new], axis=1), n_pad)
        k_all = cat(cf_k.reshape(b, p_len, FOX_WIDTH), fkb)
        v_all = cat(cf_v.reshape(b, p_len, FOX_WIDTH), fvb)
        dk_all = cat(jnp.repeat(cd_k, 2, axis=2).reshape(b, p_len, 2 * LANES), dk2)
        dv_all = cat(cd_v.reshape(b, p_len, LANES), dvb)
        ik_all = cat(jnp.concatenate([ci_k, ci_k], axis=-1), ik2)
        lf_past = jnp.pad(cf_logf.astype(F32), ((0, 0), (0, 0), (0, LANES - FOX_HEADS)))
        d_past = _cumsum(lf_past, reverse=True, inclusive=False, scale=LOG2E)
        kb = cat(d_past, neg_c_new)
    tq = min(512, t_pad)
    tk = _largest_tile(n_pad, (512, 384, 256, 128) if tq >= 512 else (1408, 1024, 512, 384, 256, 128))
    pad_q = lambda a: _pad_rows(a, t_pad)
    w_t = jnp.transpose(misc3[:, :, _MISC_IW:_MISC_IW + SUBLANES], (0, 2, 1))
    w_t = jnp.pad(w_t, ((0, 0), (0, 0), (0, t_pad - t)))
    fox_o = _fox_attn(pad_q(fq), k_all, v_all, kb, q_pos0=p_len, tq=tq, tk=tk)[:, :t]
    dsa_o = _dsa_attn(pad_q(dq), pad_q(iq), w_t, dk_all, dv_all, ik_all, q_pos0=p_len, n_real=n_keys,
                      top_k=top_k, tq=tq, tk=tk)[:, :t]
    y_out = _post(y.reshape(b * t, d), fox_o.reshape(b * t, -1), dsa_o.reshape(b * t, -1),
                  w_out, ln1_g, ln1_b, w_gate, w_up, w_down, ln2_g, ln2_b, alpha=alpha, tm=min(512, b * t))
    states = (fk.reshape(b, t, FOX_HEADS, HEAD_DIM), fv.reshape(b, t, FOX_HEADS, HEAD_DIM),
              logf.reshape(b, t, FOX_HEADS),
              dk.reshape(b, t, DSA_KV_HEADS, HEAD_DIM), dv.reshape(b, t, DSA_KV_HEADS, HEAD_DIM),
              ik.reshape(b, t, HEAD_DIM))
    return y_out.reshape(b, t, d), states


def kernel(x_prompt, x_sample, cache_fox_k, cache_fox_v, cache_fox_logf, cache_dsa_k, cache_dsa_v, cache_idx_k,
           w_in, b_forget, w_out, ln1_g, ln1_b, w_gate, w_up, w_down, ln2_g, ln2_b):
    depth = w_in.shape[0]
    alpha = (2 * depth) ** 0.25
    yp, ys = x_prompt, x_sample
    p_states, s_states = [], []
    for l in range(depth):
        w_packed, bias = _pack_w_in(w_in[l], b_forget[l])
        weights = (w_packed, bias, w_out[l], ln1_g[l], ln1_b[l], w_gate[l], w_up[l], w_down[l], ln2_g[l], ln2_b[l])
        yp, st = _mixer_layer(yp, None, weights, alpha=alpha)
        p_states.append(st)
        past = (cache_fox_k[l], cache_fox_v[l], cache_fox_logf[l], cache_dsa_k[l], cache_dsa_v[l], cache_idx_k[l])
        ys, st = _mixer_layer(ys, past, weights, alpha=alpha)
        s_states.append(st)
    stack = lambda sts: tuple(jnp.stack([s[n] for s in sts], axis=0) for n in range(6))
    return (yp, ys) + stack(p_states) + stack(s_states)
```

```python
import functools

import numpy as np
import jax
import jax.numpy as jnp
from jax import lax
from jax.experimental import pallas as pl
from jax.experimental.pallas import tpu as pltpu

HEAD_DIM = 64
FOX_HEADS = 8
DSA_HEADS = 8
DSA_KV_HEADS = 2
IDX_HEADS = 4
CHUNK = 64
CHUNK_SHIFT = 6
IDX_TOPK_MAX = 256
ROPE_THETA = 500000.0
ROT_DIM = HEAD_DIM // 4
LN_EPS = 1e-5
LOG2E = 1.4426950408889634

LANES = 128
SUBLANES = 8
FOX_WIDTH = FOX_HEADS * HEAD_DIM
DSA_WIDTH = DSA_HEADS * HEAD_DIM
INT_MIN = -(2 ** 31)
ZERO_BAND = 1 << 13
NEG_INF = float("-inf")
F32 = jnp.float32
BF16 = jnp.bfloat16
VMEM_LIMIT = 56 * 1024 * 1024

_C_FQ, _C_FK, _C_FV, _C_DQ = 0, 512, 1024, 1536
_C_DK, _C_DV, _C_IQ, _C_IK, _C_MISC, _C_END = 2048, 2176, 2304, 2560, 2688, 2816
_MISC_IW = 8


def _nt_dot(a, b):
    return lax.dot_general(a, b, (((1,), (1,)), ((), ())), preferred_element_type=F32)


def _half_mask(shape, hh):
    lane = lax.broadcasted_iota(jnp.int32, shape, len(shape) - 1)
    return (lane < HEAD_DIM) if hh == 0 else (lane >= HEAD_DIM)


def _project_kernel(x_ref, w_ref, bias_ref, c_ref, sa_ref, sb_ref,
                    fq_ref, fk_ref, fkb_ref, fv_ref, fvb_ref, dq_ref, dk_ref, dv_ref,
                    dk2_ref, dvb_ref, iq_ref, ik_ref, ik2_ref, misc_ref, logf_ref):
    xb = x_ref[...].astype(BF16)
    cos, sa, sb = c_ref[...], sa_ref[...], sb_ref[...]

    def mm(lo, hi):
        return jnp.dot(xb, w_ref[:, lo:hi], preferred_element_type=F32)

    def rope(yb):
        return yb * cos + pltpu.roll(yb, LANES - ROT_DIM // 2, 1) * sa + pltpu.roll(yb, ROT_DIM // 2, 1) * sb

    def blocks(y):
        return [y[:, b * LANES:(b + 1) * LANES] for b in range(y.shape[1] // LANES)]

    fq_ref[...] = mm(_C_FQ, _C_FK).astype(BF16)
    y = mm(_C_FK, _C_FV)
    fk_ref[...] = y
    fkb_ref[...] = y.astype(BF16)
    y = mm(_C_FV, _C_DQ)
    fv_ref[...] = y
    fvb_ref[...] = y.astype(BF16)
    for b, yb in enumerate(blocks(mm(_C_DQ, _C_DK))):
        dq_ref[:, b * LANES:(b + 1) * LANES] = rope(yb).astype(BF16)

    half0 = _half_mask((x_ref.shape[0], LANES), 0)
    kv = mm(_C_DK, _C_IQ)
    k = rope(kv[:, :LANES])
    v = kv[:, LANES:]
    for src, dst in ((k, dk_ref), (v, dv_ref)):
        for g in range(DSA_KV_HEADS):
            dst[:, g, :] = src[:, g * HEAD_DIM:(g + 1) * HEAD_DIM]
    dvb_ref[...] = v.astype(BF16)
    sw = pltpu.roll(k, HEAD_DIM, 1)
    dk2_ref[:, :LANES] = jnp.where(half0, k, sw).astype(BF16)
    dk2_ref[:, LANES:] = jnp.where(half0, sw, k).astype(BF16)

    for b, yb in enumerate(blocks(mm(_C_IQ, _C_IK))):
        iq_ref[:, b * LANES:(b + 1) * LANES] = rope(yb).astype(BF16)
    ik = rope(mm(_C_IK, _C_MISC))
    ik_ref[...] = ik[:, :HEAD_DIM]
    ik2_ref[...] = ik.astype(BF16)

    z = mm(_C_MISC, _C_END)
    zf = z + bias_ref[...]
    logf = jnp.minimum(zf, 0.0) - jnp.log1p(jnp.exp(-jnp.abs(zf)))
    lane = lax.broadcasted_iota(jnp.int32, z.shape, 1)
    misc_ref[...] = jnp.where(lane < _MISC_IW, logf, z)
    logf_ref[...] = logf[:, :FOX_HEADS]


def _pack_w_in(w, b_forget):
    d = w.shape[0]
    sizes = (FOX_WIDTH, FOX_WIDTH, FOX_WIDTH, FOX_HEADS, DSA_WIDTH, DSA_KV_HEADS * HEAD_DIM,
             DSA_KV_HEADS * HEAD_DIM, IDX_HEADS * HEAD_DIM, HEAD_DIM, IDX_HEADS)
    offs = np.concatenate([[0], np.cumsum(sizes)])
    fq, fk, fv, ff, dq, dk, dv, iq, ik, iw = [w[:, offs[i]:offs[i + 1]] for i in range(len(sizes))]
    scale = HEAD_DIM ** -0.5 * LOG2E
    misc = jnp.concatenate([ff, iw, jnp.zeros((d, LANES - FOX_HEADS - IDX_HEADS), w.dtype)], axis=1)
    packed = jnp.concatenate([fq * scale, fk, fv, dq * scale, dk, dv, iq, ik, ik, misc], axis=1)
    bias = jnp.concatenate([b_forget.astype(F32), jnp.zeros((LANES - FOX_HEADS,), F32)])[None, :]
    return packed.astype(BF16), bias


def _rope_lane_tables(positions):
    half = ROT_DIM // 2
    inv_freq = ROPE_THETA ** (-jnp.arange(half, dtype=F32) * 2.0 / ROT_DIM)
    ang = positions.astype(F32)[:, None] * inv_freq[None, :]
    cos, sin = jnp.cos(ang), jnp.sin(ang)
    n = positions.shape[0]
    rest = HEAD_DIM - ROT_DIM
    c = jnp.concatenate([cos, cos, jnp.ones((n, rest), F32)], axis=1)
    sa = jnp.concatenate([-sin, jnp.zeros((n, half + rest), F32)], axis=1)
    sb = jnp.concatenate([jnp.zeros((n, half), F32), sin, jnp.zeros((n, rest), F32)], axis=1)
    rep = LANES // HEAD_DIM
    return tuple(jnp.tile(t, (1, rep)) for t in (c, sa, sb))


def _project(x2, w_packed, bias, tables, seq, tm):
    n, d = x2.shape
    assert seq % tm == 0 and n % seq == 0
    tiles_per_seq = seq // tm
    row = lambda t: (t, 0)
    pos = lambda t: (t % tiles_per_seq, 0)
    const = lambda t: (0, 0)

    def out(width, dtype):
        return jax.ShapeDtypeStruct((n, width), dtype), pl.BlockSpec((tm, width), row)

    kv_state = (jax.ShapeDtypeStruct((n, DSA_KV_HEADS, HEAD_DIM), F32),
                pl.BlockSpec((tm, DSA_KV_HEADS, HEAD_DIM), lambda t: (t, 0, 0)))
    outs = [out(FOX_WIDTH, BF16),
            out(FOX_WIDTH, F32), out(FOX_WIDTH, BF16),
            out(FOX_WIDTH, F32), out(FOX_WIDTH, BF16),
            out(DSA_WIDTH, BF16),
            kv_state, kv_state,
            out(2 * LANES, BF16), out(LANES, BF16),
            out(IDX_HEADS * HEAD_DIM, BF16),
            out(HEAD_DIM, F32), out(LANES, BF16),
            out(LANES, F32),
            out(FOX_HEADS, F32)]
    return pl.pallas_call(
        _project_kernel,
        grid=(n // tm,),
        in_specs=[pl.BlockSpec((tm, d), row),
                  pl.BlockSpec((d, _C_END), const, pipeline_mode=pl.Buffered(1)),
                  pl.BlockSpec((1, LANES), const),
                  pl.BlockSpec((tm, LANES), pos), pl.BlockSpec((tm, LANES), pos), pl.BlockSpec((tm, LANES), pos)],
        out_specs=[o[1] for o in outs],
        out_shape=[o[0] for o in outs],
        compiler_params=pltpu.CompilerParams(dimension_semantics=("parallel",), vmem_limit_bytes=VMEM_LIMIT),
        name="project",
    )(x2, w_packed, bias, *tables)


def _cumsum_kernel(x_ref, o_ref, carry_ref, tri_ref, *, reverse, inclusive, scale):
    tm = x_ref.shape[1]

    @pl.when(pl.program_id(1) == 0)
    def _():
        carry_ref[...] = jnp.zeros_like(carry_ref)
        row = lax.broadcasted_iota(jnp.int32, (tm, tm), 0)
        col = lax.broadcasted_iota(jnp.int32, (tm, tm), 1)
        if reverse:
            keep = (col >= row) if inclusive else (col > row)
        else:
            keep = (col <= row) if inclusive else (col < row)
        tri_ref[...] = jnp.where(keep, 1.0, 0.0).astype(BF16)

    x = x_ref[0]
    hi = x.astype(BF16)
    r1 = x - hi.astype(F32)
    mid = r1.astype(BF16)
    lo = (r1 - mid.astype(F32)).astype(BF16)
    tri = tri_ref[...]
    cs = (jnp.dot(tri, hi, preferred_element_type=F32) + jnp.dot(tri, mid, preferred_element_type=F32)
          + jnp.dot(tri, lo, preferred_element_type=F32))
    carry = carry_ref[0:1, :]
    o_ref[0] = scale * (cs + carry)
    carry_ref[0:1, :] = carry + jnp.sum(x, axis=0, keepdims=True)


def _cumsum(x, *, reverse=False, inclusive=True, scale=1.0, tm=512):
    b, s, w = x.shape
    tm = min(tm, s)
    assert s % tm == 0 and w == LANES
    nt = s // tm
    idx = (lambda i, t: (i, nt - 1 - t, 0)) if reverse else (lambda i, t: (i, t, 0))
    return pl.pallas_call(
        functools.partial(_cumsum_kernel, reverse=reverse, inclusive=inclusive, scale=scale),
        grid=(b, nt),
        in_specs=[pl.BlockSpec((1, tm, w), idx)],
        out_specs=pl.BlockSpec((1, tm, w), idx),
        out_shape=jax.ShapeDtypeStruct(x.shape, F32),
        scratch_shapes=[pltpu.VMEM((SUBLANES, w), F32), pltpu.VMEM((tm, tm), BF16)],
        compiler_params=pltpu.CompilerParams(dimension_semantics=("parallel", "arbitrary")),
        name="cumsum",
    )(x)


BIAS_PIECES = 3


def _bias_lanes(h):
    base = HEAD_DIM if h % 2 == 0 else 0
    return tuple(base + n * FOX_HEADS for n in range(BIAS_PIECES))


def _is_bias_lane(lane, h):
    hit = lane == _bias_lanes(h)[0]
    for dst in _bias_lanes(h)[1:]:
        hit = jnp.logical_or(hit, lane == dst)
    return hit


def _mask_heads(q_ref, qm_ref, n_heads, bias_ones=False):
    for h in range(n_heads):
        qpair = q_ref[0, :, (h // 2) * LANES:(h // 2 + 1) * LANES]
        other = jnp.zeros_like(qpair)
        if bias_ones:
            lane = lax.broadcasted_iota(jnp.int32, qpair.shape, 1)
            other = jnp.where(_is_bias_lane(lane, h), 1.0, 0.0).astype(qpair.dtype)
        qm_ref[h] = jnp.where(_half_mask(qpair.shape, h % 2), qpair, other)


ONES_ROWS = 16
VAL_ROWS = HEAD_DIM + ONES_ROWS


def _flash_scratch(n_heads, tq, tk):
    return [pltpu.VMEM((2, n_heads, tk, tq), F32), pltpu.VMEM((2, n_heads, SUBLANES, tq), F32),
            pltpu.VMEM((n_heads, 1, tq), F32), pltpu.VMEM((n_heads, VAL_ROWS, tq), F32)]


def _init_softmax(m_ref, acc_ref):
    m_ref[...] = jnp.full_like(m_ref, NEG_INF)
    acc_ref[...] = jnp.zeros_like(acc_ref)


def _flash_loop(segments, logits0, values, s_ref, st_ref, m_ref, acc_ref):
    def by_parity(j, fn):
        for slot in range(2):
            pl.when(lax.rem(j, 2) == slot)(functools.partial(fn, slot))

    n_heads = m_ref.shape[0]

    def stage_a(h, s, slot):
        s_ref[slot, h] = s
        m_prev = m_ref[h]
        m_new = jnp.maximum(m_prev, jnp.max(s, axis=0, keepdims=True))
        m_use = jnp.where(m_new == NEG_INF, 0.0, m_new)
        st_ref[slot, h, 0:1, :] = m_use
        st_ref[slot, h, 1:2, :] = jnp.exp2(m_prev - m_use)
        m_ref[h] = m_new

    def stage_b(j, h, slot):
        p = jnp.exp2(s_ref[slot, h] - st_ref[slot, h, 0:1, :]).astype(BF16)
        alpha = st_ref[slot, h, 1:2, :]
        acc_ref[h] = alpha * acc_ref[h] + jnp.dot(values(j, h), p, preferred_element_type=F32)

    for h, s in enumerate(logits0(0)):
        stage_a(h, s, 0)
    start = 1
    for end, logits in segments:
        def both(slot, j, logits):
            tiles = logits(j)
            for h in range(n_heads):
                stage_b(j - 1, h, 1 - slot)
                stage_a(h, next(tiles), slot)

        def body(j, carry, logits=logits):
            by_parity(j, functools.partial(both, j=j, logits=logits))
            return carry

        lax.fori_loop(start, end, body, 0)
        start = jnp.maximum(start, end)
    def drain(slot):
        for h in range(n_heads):
            stage_b(start - 1, h, slot)

    by_parity(start - 1, drain)


def _store_heads(o_ref, acc_ref, n_heads):
    def head(h):
        return acc_ref[h, :HEAD_DIM, :] / acc_ref[h, HEAD_DIM:HEAD_DIM + 1, :]

    for c in range(n_heads // 2):
        ot = jnp.concatenate([head(2 * c), head(2 * c + 1)], axis=0)
        o_ref[0, :, c * LANES:(c + 1) * LANES] = ot.T.astype(o_ref.dtype)


def _key_major(v, tk):
    b, t, c = v.shape
    heads = c // HEAD_DIM
    vt = jnp.transpose(v.reshape(b, t // tk, tk, heads, HEAD_DIM), (0, 1, 3, 4, 2))
    ones = jnp.ones((b, t // tk, heads, ONES_ROWS, tk), v.dtype)
    return jnp.concatenate([vt, ones], axis=3).reshape(b, t // tk, heads * VAL_ROWS, tk)


def _fox_keys_kernel(k_ref, kb_ref, o_ref):
    kb = kb_ref[0]
    hi = kb.astype(BF16).astype(F32)
    r1 = kb - hi
    mid = r1.astype(BF16).astype(F32)
    lane = lax.broadcasted_iota(jnp.int32, kb.shape, 1)
    packed = jnp.zeros_like(kb)
    for n, piece in enumerate((hi, mid, r1 - mid)):
        in_group = jnp.logical_and(lane >= n * FOX_HEADS, lane < (n + 1) * FOX_HEADS)
        packed = jnp.where(in_group, pltpu.roll(piece, n * FOX_HEADS, 1) if n else piece, packed)
    for h in range(FOX_HEADS):
        placed = pltpu.roll(packed, (_bias_lanes(h)[0] - h) % LANES, 1)
        bias = jnp.where(_is_bias_lane(lane, h), placed, 0.0).astype(BF16)
        kpair = k_ref[0, :, (h // 2) * LANES:(h // 2 + 1) * LANES]
        o_ref[0, :, h * LANES:(h + 1) * LANES] = jnp.where(_half_mask(kpair.shape, h % 2), kpair, bias)


def _fox_keys(k, kb, tm=512):
    b, t, _ = k.shape
    tm = _largest_tile(t, (tm, 384, 256, 128))
    spec = lambda w: pl.BlockSpec((1, tm, w), lambda bb, i: (bb, i, 0))
    return pl.pallas_call(
        _fox_keys_kernel,
        grid=(b, t // tm),
        in_specs=[spec(FOX_WIDTH), spec(LANES)],
        out_specs=spec(FOX_HEADS * LANES),
        out_shape=jax.ShapeDtypeStruct((b, t, FOX_HEADS * LANES), BF16),
        compiler_params=pltpu.CompilerParams(dimension_semantics=("parallel", "parallel")),
        name="fox_keys",
    )(k, kb)


def _fox_kernel(q_ref, k_ref, vt_ref, o_ref, qm_ref, s_ref, st_ref, m_ref, acc_ref, *, tq, tk, q_pos0, nk):
    row0 = q_pos0 + pl.program_id(1) * tq
    n_full = jnp.minimum(nk, (row0 + 1) // tk)
    j_end = jnp.minimum(nk, (row0 + tq - 1) // tk + 1)
    _mask_heads(q_ref, qm_ref, FOX_HEADS, bias_ones=True)
    _init_softmax(m_ref, acc_ref)

    def logits(j, masked):
        off = pl.multiple_of(j * tk, tk)
        if masked:
            key_pos = j * tk + lax.broadcasted_iota(jnp.int32, (tk, tq), 0)
            visible = key_pos <= row0 + lax.broadcasted_iota(jnp.int32, (tk, tq), 1)
        for h in range(FOX_HEADS):
            s = _nt_dot(k_ref[0, pl.ds(off, tk), h * LANES:(h + 1) * LANES], qm_ref[h])
            yield jnp.where(visible, s, NEG_INF) if masked else s

    def values(j, h):
        return vt_ref[0, j, h * VAL_ROWS:(h + 1) * VAL_ROWS, :]

    masked = functools.partial(logits, masked=True)
    _flash_loop([(n_full, functools.partial(logits, masked=False)), (j_end, masked)], masked, values,
                s_ref, st_ref, m_ref, acc_ref)
    _store_heads(o_ref, acc_ref, FOX_HEADS)


def _fox_attn(q, k, v, kb, *, q_pos0, tq, tk):
    b, tq_all, _ = q.shape
    tk_all = k.shape[1]
    assert tq_all % tq == 0 and tk_all % tk == 0 and tq % LANES == 0
    nk = tk_all // tk
    vt = _key_major(v, tk)
    k_aug = _fox_keys(k, kb)
    return pl.pallas_call(
        functools.partial(_fox_kernel, tq=tq, tk=tk, q_pos0=q_pos0, nk=nk),
        grid=(b, tq_all // tq),
        in_specs=[pl.BlockSpec((1, tq, FOX_WIDTH), lambda bb, i: (bb, i, 0)),
                  pl.BlockSpec((1, tk_all, FOX_HEADS * LANES), lambda bb, i: (bb, 0, 0)),
                  pl.BlockSpec((1, nk, FOX_HEADS * VAL_ROWS, tk), lambda bb, i: (bb, 0, 0, 0))],
        out_specs=pl.BlockSpec((1, tq, FOX_WIDTH), lambda bb, i: (bb, i, 0)),
        out_shape=jax.ShapeDtypeStruct(q.shape, BF16),
        scratch_shapes=[pltpu.VMEM((FOX_HEADS, tq, LANES), BF16)] + _flash_scratch(FOX_HEADS, tq, tk),
        compiler_params=pltpu.CompilerParams(dimension_semantics=("parallel", "arbitrary"),
                                             vmem_limit_bytes=VMEM_LIMIT),
        name="fox_attn",
    )(q, k_aug, vt)


def _dsa_kernel(dq_ref, iq_ref, w_ref, dk_ref, dvt_ref, ik_ref, o_ref,
                qm_ref, iqm_ref, key_ref, hi_ref, lo_ref, drop_ref, s_ref, st_ref, m_ref, acc_ref,
                *, tq, tk, q_pos0, n_real, top_k, nk):
    row0 = q_pos0 + pl.program_id(1) * tq
    q_pos = row0 + lax.broadcasted_iota(jnp.int32, (1, tq), 1)
    lim = jnp.minimum(lax.shift_left(jnp.right_shift(q_pos, CHUNK_SHIFT) + 1, CHUNK_SHIFT), n_real)
    lim_first = jnp.minimum((row0 // CHUNK + 1) * CHUNK, n_real)
    lim_last = jnp.minimum(((row0 + tq - 1) // CHUNK + 1) * CHUNK, n_real)
    n_full = lim_first // tk
    nt = (lim_last + tk - 1) // tk
    _mask_heads(dq_ref, qm_ref, DSA_HEADS)
    _mask_heads(iq_ref, iqm_ref, IDX_HEADS)

    def ktile(ref, j, lo, hi):
        return ref[0, pl.ds(pl.multiple_of(j * tk, tk), tk), lo:hi]

    def key_pos(j):
        return j * tk + lax.broadcasted_iota(jnp.int32, (tk, tq), 0)

    def score_tile(j, edge):
        ikt = ktile(ik_ref, j, 0, LANES)
        sc = jnp.zeros((tk, tq), F32)
        for h in range(IDX_HEADS):
            sc = sc + w_ref[0, h:h + 1, :] * jnp.maximum(_nt_dot(ikt, iqm_ref[h]), 0.0)
        bits = pltpu.bitcast(sc, jnp.int32)
        key = jnp.where(bits < 0, (bits ^ 0x7FFFFFFF) + 1 - ZERO_BAND, bits)
        pos = key_pos(j)
        key = jnp.where(sc == 0.0, -1 - pos, key)
        if edge:
            key = jnp.where(pos < lim, key, INT_MIN)
        key_ref[j] = key
        hi_ref[j] = jnp.right_shift(key, 16).astype(jnp.int16)
        lo_ref[j] = (jnp.bitwise_and(key, 0xFFFF) - 32768).astype(jnp.int16)

    def score_full(j, carry):
        score_tile(j, False)
        return carry

    def score_edge(j, carry):
        score_tile(j, True)
        return carry

    lax.fori_loop(0, n_full, score_full, 0)
    lax.fori_loop(n_full, nt, score_edge, 0)

    def count16(ref, cand):
        cand16 = cand.astype(jnp.int16)

        def body(j, acc):
            hit = jnp.where(ref[j] >= cand16, jnp.asarray(1, BF16), jnp.asarray(0, BF16))
            n_acc = max(1, 4 * LANES // tq)
            parts = [hit[g * 16:(g + 1) * 16, :] for g in range(n_acc)]
            for g in range(n_acc, tk // 16):
                parts[g % n_acc] = parts[g % n_acc] + hit[g * 16:(g + 1) * 16, :]
            part = parts[0]
            for extra in parts[1:]:
                part = part + extra
            return acc + part.astype(F32)

        acc = lax.fori_loop(0, nt, body, jnp.zeros((16, tq), F32))
        return jnp.sum(acc, axis=0, keepdims=True)

    def bisect16(ref, base, cnt0):
        def step(it, carry):
            t, cnt_t = carry
            cand = t + lax.shift_left(jnp.int32(1), 15 - it)
            cnt = base + count16(ref, cand)
            ok = cnt >= top_k
            return jnp.where(ok, cand, t), jnp.where(ok, cnt, cnt_t)
        return lax.fori_loop(0, 16, step, (jnp.full((1, tq), -32768, jnp.int32), cnt0))

    t_hi, cnt_hi = bisect16(hi_ref, 0.0, jnp.full((1, tq), 2.0 ** 30, F32))
    above = jnp.where(t_hi >= 32767, 0.0, count16(hi_ref, jnp.minimum(t_hi + 1, 32767)))
    t_hi16 = t_hi.astype(jnp.int16)

    def keep_low(j, carry):
        lo_ref[j] = jnp.where(hi_ref[j] == t_hi16, lo_ref[j], jnp.asarray(-32768, jnp.int16))
        return carry

    lax.fori_loop(0, nt, keep_low, 0)
    t_lo, cnt_thr = bisect16(lo_ref, above, cnt_hi)
    thr = t_hi * 65536 + (t_lo + 32768)

    def count(pred):
        def body(j, acc):
            hit = jnp.where(pred(key_ref[j], j), 1.0, 0.0)
            return acc + jnp.sum(hit.reshape(tk // SUBLANES, SUBLANES, tq), axis=0)
        acc = lax.fori_loop(0, nt, body, jnp.zeros((SUBLANES, tq), F32))
        return jnp.sum(acc, axis=0, keepdims=True)

    excess = jnp.logical_and(cnt_thr > top_k, thr > INT_MIN)

    @pl.when(jnp.max(jnp.where(excess, 1.0, 0.0)) > 0.0)
    def _():
        need = top_k - count(lambda kt, j: kt > thr)
        bound = jnp.zeros((1, tq), jnp.int32)
        for bit in reversed(range(int(nk * tk - 1).bit_length())):
            cand = bound + (1 << bit)
            c = count(lambda kt, j: jnp.logical_and(kt == thr, key_pos(j) < cand))
            bound = jnp.where(c < need, cand, bound)

        def demote(j, carry):
            kt = key_ref[j]
            drop = jnp.logical_and(jnp.logical_and(kt == thr, key_pos(j) > bound), excess)
            key_ref[j] = jnp.where(drop, kt - 1, kt)
            return carry

        lax.fori_loop(0, nt, demote, 0)

    thr_sel = jnp.maximum(thr, INT_MIN + 1)
    _init_softmax(m_ref, acc_ref)
    kv_head = lambda h: h // (DSA_HEADS // DSA_KV_HEADS)

    def logits(j):
        drop_ref[...] = jnp.where(key_ref[j] >= thr_sel, 0.0, NEG_INF)
        for h in range(DSA_HEADS):
            g = kv_head(h)
            yield _nt_dot(ktile(dk_ref, j, g * LANES, (g + 1) * LANES), qm_ref[h]) + drop_ref[...]

    def values(j, h):
        return dvt_ref[0, j, kv_head(h) * VAL_ROWS:(kv_head(h) + 1) * VAL_ROWS, :]

    _flash_loop([(nt, logits)], logits, values, s_ref, st_ref, m_ref, acc_ref)
    _store_heads(o_ref, acc_ref, DSA_HEADS)


def _dsa_attn(dq, iq, w_t, dk2, dv, ik2, *, q_pos0, n_real, top_k, tq, tk):
    b, tq_all, _ = dq.shape
    tk_all = dk2.shape[1]
    assert tq_all % tq == 0 and tk_all % tk == 0 and tq % LANES == 0 and tk % 16 == 0 and tk_all <= ZERO_BAND
    nk = tk_all // tk
    dvt = _key_major(dv, tk)
    qspec = lambda w: pl.BlockSpec((1, tq, w), lambda bb, i: (bb, i, 0))
    kspec = lambda w: pl.BlockSpec((1, tk_all, w), lambda bb, i: (bb, 0, 0))
    return pl.pallas_call(
        functools.partial(_dsa_kernel, tq=tq, tk=tk, q_pos0=q_pos0, n_real=n_real, top_k=top_k, nk=nk),
        grid=(b, tq_all // tq),
        in_specs=[qspec(DSA_WIDTH), qspec(IDX_HEADS * HEAD_DIM),
                  pl.BlockSpec((1, SUBLANES, tq), lambda bb, i: (bb, 0, i)),
                  kspec(2 * LANES),
                  pl.BlockSpec((1, nk, DSA_KV_HEADS * VAL_ROWS, tk), lambda bb, i: (bb, 0, 0, 0)),
                  kspec(LANES)],
        out_specs=qspec(DSA_WIDTH),
        out_shape=jax.ShapeDtypeStruct(dq.shape, BF16),
        scratch_shapes=[pltpu.VMEM((DSA_HEADS, tq, LANES), BF16), pltpu.VMEM((IDX_HEADS, tq, LANES), BF16),
                        pltpu.VMEM((nk, tk, tq), jnp.int32), pltpu.VMEM((nk, tk, tq), jnp.int16),
                        pltpu.VMEM((nk, tk, tq), jnp.int16), pltpu.VMEM((tk, tq), F32)]
                       + _flash_scratch(DSA_HEADS, tq, tk),
        compiler_params=pltpu.CompilerParams(dimension_semantics=("parallel", "arbitrary"),
                                             vmem_limit_bytes=VMEM_LIMIT),
        name="dsa_attn",
    )(dq, iq, w_t, dk2, dvt, ik2)


def _layer_norm(x, g, b):
    mu = jnp.mean(x, axis=-1, keepdims=True)
    xc = x - mu
    var = jnp.mean(xc * xc, axis=-1, keepdims=True)
    return xc * lax.rsqrt(var + LN_EPS) * g + b


def _post_kernel(x_ref, fox_ref, dsa_ref, wo_ref, g1_ref, b1_ref, wg_ref, wu_ref, wd_ref, g2_ref, b2_ref,
                 o_ref, *, alpha, ff_chunk):
    mix = (jnp.dot(fox_ref[...], wo_ref[:FOX_WIDTH, :], preferred_element_type=F32)
           + jnp.dot(dsa_ref[...], wo_ref[FOX_WIDTH:, :], preferred_element_type=F32))
    h = _layer_norm(alpha * x_ref[...] + mix, g1_ref[...], b1_ref[...])
    hb = h.astype(BF16)
    f = jnp.zeros_like(h)
    for c in range(wg_ref.shape[1] // ff_chunk):
        lo, hi = c * ff_chunk, (c + 1) * ff_chunk
        gate = jnp.dot(hb, wg_ref[:, lo:hi], preferred_element_type=F32)
        up = jnp.dot(hb, wu_ref[:, lo:hi], preferred_element_type=F32)
        act = (gate * jax.nn.sigmoid(gate) * up).astype(BF16)
        f = f + jnp.dot(act, wd_ref[lo:hi, :], preferred_element_type=F32)
    o_ref[...] = _layer_norm(alpha * h + f, g2_ref[...], b2_ref[...])


def _post(x2, fox_o, dsa_o, w_out, ln1_g, ln1_b, w_gate, w_up, w_down, ln2_g, ln2_b, *, alpha, tm, ff_chunk=256):
    n, d = x2.shape
    d_ff = w_gate.shape[1]
    assert n % tm == 0 and d_ff % ff_chunk == 0
    row = lambda t: (t, 0)
    const = lambda t: (0, 0)
    wspec = lambda shape: pl.BlockSpec(shape, const, pipeline_mode=pl.Buffered(1))
    vec = lambda a: a.astype(F32)[None, :]
    return pl.pallas_call(
        functools.partial(_post_kernel, alpha=alpha, ff_chunk=ff_chunk),
        grid=(n // tm,),
        in_specs=[pl.BlockSpec((tm, d), row), pl.BlockSpec((tm, FOX_WIDTH), row), pl.BlockSpec((tm, DSA_WIDTH), row),
                  wspec(w_out.shape), wspec((1, d)), wspec((1, d)),
                  wspec(w_gate.shape), wspec(w_up.shape), wspec(w_down.shape), wspec((1, d)), wspec((1, d))],
        out_specs=pl.BlockSpec((tm, d), row),
        out_shape=jax.ShapeDtypeStruct((n, d), F32),
        compiler_params=pltpu.CompilerParams(dimension_semantics=("parallel",), vmem_limit_bytes=VMEM_LIMIT),
        name="post",
    )(x2, fox_o, dsa_o, w_out.astype(BF16), vec(ln1_g), vec(ln1_b),
      w_gate.astype(BF16), w_up.astype(BF16), w_down.astype(BF16), vec(ln2_g), vec(ln2_b))


def _round_up(n, m):
    return -(-n // m) * m


def _pad_rows(a, rows):
    return jnp.pad(a, ((0, 0), (0, rows - a.shape[1]), (0, 0)))


def _largest_tile(n, candidates):
    return next(c for c in candidates if n % c == 0)


def _mixer_layer(y, past, weights, *, alpha):
    (w_packed, bias, w_out, ln1_g, ln1_b, w_gate, w_up, w_down, ln2_g, ln2_b) = weights
    b, t, d = y.shape
    p_len = 0 if past is None else past[0].shape[1]
    tables = _rope_lane_tables(p_len + jnp.arange(t))
    (fq, fk, fkb, fv, fvb, dq, dk, dv, dk2, dvb, iq, ik, ik2, misc, logf) = _project(
        y.reshape(b * t, d), w_packed, bias, tables, t, min(512, t))
    r3 = lambda a: a.reshape(b, t, a.shape[-1])
    fq, fkb, fvb, dq, dk2, dvb, iq, ik2, misc3 = map(r3, (fq, fkb, fvb, dq, dk2, dvb, iq, ik2, misc))

    t_pad = _round_up(t, LANES)
    neg_c_new = _cumsum(_pad_rows(misc3, t_pad), scale=-LOG2E)[:, :t]
    n_keys = p_len + t
    top_k = min(IDX_TOPK_MAX, n_keys // 4)
    if past is None:
        n_pad = n_keys
        k_all, v_all, kb = fkb, fvb, neg_c_new
        dk_all, dv_all, ik_all = dk2, dvb, ik2
    else:
        cf_k, cf_v, cf_logf, cd_k, cd_v, ci_k = past
        n_pad = _round_up(n_keys, LANES)
        cat = lambda old, new: _pad_rows(jnp.concatenate([old.astype(new.dtype), new], axis=1), n_pad)
        k_all = cat(cf_k.reshape(b, p_len, FOX_WIDTH), fkb)
        v_all = cat(cf_v.reshape(b, p_len, FOX_WIDTH), fvb)
        dk_all = cat(jnp.repeat(cd_k, 2, axis=2).reshape(b, p_len, 2 * LANES), dk2)
        dv_all = cat(cd_v.reshape(b, p_len, LANES), dvb)
        ik_all = cat(jnp.concatenate([ci_k, ci_k], axis=-1), ik2)
        lf_past = jnp.pad(cf_logf.astype(F32), ((0, 0), (0, 0), (0, LANES - FOX_HEADS)))
        d_past = _cumsum(lf_past, reverse=True, inclusive=False, scale=LOG2E)
        kb = cat(d_past, neg_c_new)
    tq = min(512, t_pad)
    tk = _largest_tile(n_pad, (512, 384, 256, 128) if tq >= 512 else (1408, 1024, 512, 384, 256, 128))
    pad_q = lambda a: _pad_rows(a, t_pad)
    w_t = jnp.transpose(misc3[:, :, _MISC_IW:_MISC_IW + SUBLANES], (0, 2, 1))
    w_t = jnp.pad(w_t, ((0, 0), (0, 0), (0, t_pad - t)))
    fox_o = _fox_attn(pad_q(fq), k_all, v_all, kb, q_pos0=p_len, tq=tq, tk=tk)[:, :t]
    dsa_o = _dsa_attn(pad_q(dq), pad_q(iq), w_t, dk_all, dv_all, ik_all, q_pos0=p_len, n_real=n_keys,
                      top_k=top_k, tq=tq, tk=tk)[:, :t]
    y_out = _post(y.reshape(b * t, d), fox_o.reshape(b * t, -1), dsa_o.reshape(b * t, -1),
                  w_out, ln1_g, ln1_b, w_gate, w_up, w_down, ln2_g, ln2_b, alpha=alpha, tm=min(512, b * t))
    states = (fk.reshape(b, t, FOX_HEADS, HEAD_DIM), fv.reshape(b, t, FOX_HEADS, HEAD_DIM),
              logf.reshape(b, t, FOX_HEADS),
              dk.reshape(b, t, DSA_KV_HEADS, HEAD_DIM), dv.reshape(b, t, DSA_KV_HEADS, HEAD_DIM),
              ik.reshape(b, t, HEAD_DIM))
    return y_out.reshape(b, t, d), states


def kernel(x_prompt, x_sample, cache_fox_k, cache_fox_v, cache_fox_logf, cache_dsa_k, cache_dsa_v, cache_idx_k,
           w_in, b_forget, w_out, ln1_g, ln1_b, w_gate, w_up, w_down, ln2_g, ln2_b):
    depth = w_in.shape[0]
    alpha = (2 * depth) ** 0.25
    yp, ys = x_prompt, x_sample
    p_states, s_states = [], []
    for l in range(depth):
        w_packed, bias = _pack_w_in(w_in[l], b_forget[l])
        weights = (w_packed, bias, w_out[l], ln1_g[l], ln1_b[l], w_gate[l], w_up[l], w_down[l], ln2_g[l], ln2_b[l])
        yp, st = _mixer_layer(yp, None, weights, alpha=alpha)
        p_states.append(st)
        past = (cache_fox_k[l], cache_fox_v[l], cache_fox_logf[l], cache_dsa_k[l], cache_dsa_v[l], cache_idx_k[l])
        ys, st = _mixer_layer(ys, past, weights, alpha=alpha)
        s_states.append(st)
    stack = lambda sts: tuple(jnp.stack([s[n] for s in sts], axis=0) for n in range(6))
    return (yp, ys) + stack(p_states) + stack(s_states)
```
